```python
import jax, jax.numpy as jnp
from jax import lax
import numpy as np

D_MODEL = 1024
BATCH = 8
SEQ = 4096
DEPTH = 2

HEAD_DIM = 64
ATT_HEADS = 8
ATT_KV_HEADS = 2
ATT_GROUP = ATT_HEADS // ATT_KV_HEADS
ATT_WIDTH = ATT_HEADS * HEAD_DIM
KV_WIDTH = ATT_KV_HEADS * HEAD_DIM
WINDOW = 128
BLOCK = 128
RWKV_HEADS = 8
RWKV_WIDTH = RWKV_HEADS * HEAD_DIM
DECAY_LORA = 64
AAA_LORA = 64
MV_LORA = 32
GATE_LORA = 128
MIX_WIDTH = ATT_WIDTH + RWKV_WIDTH
ATT_IN = ATT_WIDTH + 2 * KV_WIDTH
RWKV_IN_SIZES = (RWKV_WIDTH, RWKV_WIDTH, RWKV_WIDTH, DECAY_LORA, AAA_LORA, GATE_LORA)
SHIFT_WIDTH = sum(RWKV_IN_SIZES)
IN_WIDTH = ATT_IN + SHIFT_WIDTH
D_FF = 2816
CONV_WIDTH = 3
RMS_EPS = 1e-5
LNX_EPS = 64e-5
NEG_INF = -1e30

kernel_name = "hymba_swa_sink_alibi_rwkv7_convglu"


def _split(z, sizes):
    idx = [int(i) for i in np.cumsum(sizes)[:-1]]
    return jnp.split(z, idx, axis=-1)


def rms_norm(x, g):
    xf = x.astype(jnp.float32)
    y = xf * lax.rsqrt(jnp.mean(xf * xf, axis=-1, keepdims=True) + RMS_EPS)
    return (y * g.astype(jnp.float32)).astype(x.dtype)


def alibi_slopes(n_heads):
    return 2.0 ** (-8.0 * jnp.arange(1, n_heads + 1, dtype=jnp.float32) / n_heads)


def sliding_window_attention(q, k, v, sinks):
    dt = q.dtype
    B, T, _ = q.shape
    nb = T // BLOCK
    f32 = jnp.float32
    qb = q.astype(f32).reshape(B, nb, BLOCK, ATT_KV_HEADS, ATT_GROUP, HEAD_DIM)
    kc = k.astype(f32).reshape(B, nb, BLOCK, ATT_KV_HEADS, HEAD_DIM)
    vc = v.astype(f32).reshape(B, nb, BLOCK, ATT_KV_HEADS, HEAD_DIM)
    pad = ((0, 0), (1, 0), (0, 0), (0, 0), (0, 0))
    kb = jnp.concatenate([jnp.pad(kc, pad)[:, :-1], kc], axis=2)
    vb = jnp.concatenate([jnp.pad(vc, pad)[:, :-1], vc], axis=2)
    s = jnp.einsum('bnqkgd,bnskd->bnkgqs', qb, kb) * (HEAD_DIM ** -0.5)
    qpos = jnp.arange(BLOCK)[:, None]
    kpos = jnp.arange(2 * BLOCK)[None, :]
    dist = qpos + BLOCK - kpos
    key_abs = jnp.arange(nb)[:, None, None] * BLOCK - BLOCK + kpos[None]
    valid = (dist >= 0)[None] & (dist < WINDOW)[None] & (key_abs >= 0)
    slopes = alibi_slopes(ATT_HEADS).reshape(ATT_KV_HEADS, ATT_GROUP)
    s = s - slopes[:, :, None, None] * dist.astype(f32)
    s = jnp.where(valid[None, :, None, None], s, NEG_INF)
    sink = sinks.astype(f32).reshape(ATT_KV_HEADS, ATT_GROUP)[None, None, :, :, None, None]
    m = jnp.maximum(jnp.max(s, axis=-1, keepdims=True), sink)
    p = jnp.exp(s - m)
    denom = jnp.sum(p, axis=-1, keepdims=True) + jnp.exp(sink - m)
    o = jnp.einsum('bnkgqs,bnskd->bnqkgd', p / denom, vb)
    return o.reshape(B, T, ATT_WIDTH).astype(dt)


def token_shift(u, mu):
    prev = jnp.pad(u, ((0, 0), (1, 0), (0, 0)))[:, :-1]
    return u + (prev - u) * mu


def rwkv7_time_mix(r, k, v, wd, ad, gd, w0, w2, a0, a2, g2, k_k, k_a, r_k,
                   lnx_g, lnx_b, v_first, v_mix):
    dt = r.dtype
    f32 = jnp.float32
    B, T, C = r.shape
    H, N = RWKV_HEADS, HEAD_DIM
    w = -jax.nn.softplus(-(w0 + jnp.tanh(wd) @ w2)) - 0.5
    decay = jnp.exp(-jnp.exp(w.astype(f32)))
    a = jax.nn.sigmoid(a0 + ad @ a2)
    g = jax.nn.sigmoid(gd) @ g2
    kk = (k * k_k).astype(f32).reshape(B, T, H, N)
    kk = kk / jnp.maximum(jnp.linalg.norm(kk, axis=-1, keepdims=True), 1e-12)
    kk = kk.reshape(B, T, C)
    k = k * (1 + (a - 1) * k_a)
    if v_mix is None:
        v_first = v
    else:
        v0, v1, v2 = v_mix
        v = v + (v_first - v) * jax.nn.sigmoid(v0 + (v @ v1) @ v2)

    def heads(z):
        return z.astype(f32).reshape(B, T, H, N).transpose(1, 0, 2, 3)

    xs = (heads(r), heads(decay), heads(k), heads(v), heads(-kk), heads(kk * a))

    def step(S, inp):
        r_t, w_t, k_t, v_t, a_t, b_t = inp
        sa = jnp.einsum('bhvk,bhk->bhv', S, a_t)
        S = S * w_t[:, :, None, :] + sa[..., None] * b_t[:, :, None, :] \
            + v_t[..., None] * k_t[:, :, None, :]
        return S, jnp.einsum('bhvk,bhk->bhv', S, r_t)

    S0 = jnp.zeros((B, H, N, N), f32)
    _, y = lax.scan(step, S0, xs)
    y = y.transpose(1, 0, 2, 3)
    mu = jnp.mean(y, axis=-1, keepdims=True)
    var = jnp.mean(jnp.square(y - mu), axis=-1, keepdims=True)
    y = ((y - mu) * lax.rsqrt(var + LNX_EPS)).reshape(B, T, C)
    y = y * lnx_g.astype(f32) + lnx_b.astype(f32)
    r4 = r.astype(f32).reshape(B, T, H, N)
    k4 = k.astype(f32).reshape(B, T, H, N)
    v4 = v.astype(f32).reshape(B, T, H, N)
    bonus = jnp.sum(r4 * k4 * r_k.astype(f32).reshape(H, N), axis=-1, keepdims=True) * v4
    y = (y + bonus.reshape(B, T, C)) * g.astype(f32)
    return y.astype(dt), v_first


def conv_glu_ffn(h, w_gate, w_up, conv_w, conv_b, w_down):
    T = h.shape[1]
    u = h @ w_gate
    up = jnp.pad(u, ((0, 0), (CONV_WIDTH - 1, 0), (0, 0)))
    c = conv_b + sum(conv_w[j] * up[:, j:j + T] for j in range(CONV_WIDTH))
    return (jax.nn.silu(c) * (h @ w_up)) @ w_down


def setup_inputs(seed: int = 0) -> dict:
    key = jax.random.key(seed)
    ks = iter(jax.random.split(key, 40))
    f32 = jnp.float32
    L, C, D, F = DEPTH, RWKV_WIDTH, D_MODEL, D_FF

    def nrm(shape, scale):
        return jax.random.normal(next(ks), shape, f32) * scale

    def unif(shape, lo, hi):
        return jax.random.uniform(next(ks), shape, f32, minval=lo, maxval=hi)

    return {
        "x": nrm((BATCH, SEQ, D), 1.0),
        "norm1_g": 1.0 + nrm((L, D), 0.01),
        "w_in": nrm((L, D, IN_WIDTH), D ** -0.5),
        "attn_sinks": nrm((L, ATT_HEADS), 0.5),
        "shift_mu": unif((L, SHIFT_WIDTH), 0.0, 1.0),
        "w0": unif((L, C), -5.0, 1.0),
        "w2": nrm((L, DECAY_LORA, C), 0.1),
        "a0": nrm((L, C), 0.1),
        "a2": nrm((L, AAA_LORA, C), 0.1),
        "g2": nrm((L, GATE_LORA, C), GATE_LORA ** -0.5),
        "k_k": 0.85 + nrm((L, C), 0.05),
        "k_a": 1.0 + nrm((L, C), 0.05),
        "r_k": nrm((L, C), 0.1),
        "lnx_g": 1.0 + nrm((L, C), 0.01),
        "lnx_b": nrm((L, C), 0.01),
        "v0": nrm((L - 1, C), 0.1),
        "v1": nrm((L - 1, C, MV_LORA), C ** -0.5),
        "v2": nrm((L - 1, MV_LORA, C), 0.1),
        "w_out": nrm((L, MIX_WIDTH, D), MIX_WIDTH ** -0.5),
        "norm2_g": 1.0 + nrm((L, D), 0.01),
        "ffn_w_gate": nrm((L, D, F), D ** -0.5),
        "ffn_w_up": nrm((L, D, F), D ** -0.5),
        "conv_w": nrm((L, CONV_WIDTH, F), 0.5),
        "conv_b": nrm((L, F), 0.01),
        "ffn_w_down": nrm((L, F, D), F ** -0.5),
        "final_g": 1.0 + nrm((D,), 0.01),
    }


def reference(x, norm1_g, w_in, attn_sinks, shift_mu, w0, w2, a0, a2, g2, k_k, k_a,
              r_k, lnx_g, lnx_b, v0, v1, v2, w_out, norm2_g, ffn_w_gate, ffn_w_up,
              conv_w, conv_b, ffn_w_down, final_g):
    v_first = None
    for l in range(DEPTH):
        h = rms_norm(x, norm1_g[l])
        proj = h @ w_in[l]
        q, ka, va = _split(proj[..., :ATT_IN], (ATT_WIDTH, KV_WIDTH, KV_WIDTH))
        rest = token_shift(proj[..., ATT_IN:], shift_mu[l])
        r, k, v, wd, ad, gd = _split(rest, RWKV_IN_SIZES)
        att = sliding_window_attention(q, ka, va, attn_sinks[l])
        v_mix = None if l == 0 else (v0[l - 1], v1[l - 1], v2[l - 1])
        rw, v_first = rwkv7_time_mix(r, k, v, wd, ad, gd, w0[l], w2[l], a0[l], a2[l],
                                     g2[l], k_k[l], k_a[l], r_k[l], lnx_g[l], lnx_b[l],
                                     v_first, v_mix)
        x = x + jnp.concatenate([att, rw], axis=-1) @ w_out[l]
        h = rms_norm(x, norm2_g[l])
        x = x + conv_glu_ffn(h, ffn_w_gate[l], ffn_w_up[l], conv_w[l], conv_b[l], ffn_w_down[l])
    return rms_norm(x, final_g)
```

```python
import functools

import jax
import jax.numpy as jnp
from jax import lax
from jax.experimental import pallas as pl
from jax.experimental.pallas import tpu as pltpu

F32 = jnp.float32
BF16 = jnp.bfloat16
HIGHEST = lax.Precision.HIGHEST

HEAD_DIM = 64
ATT_HEADS = 8
ATT_KV_HEADS = 2
ATT_GROUP = ATT_HEADS // ATT_KV_HEADS
ATT_WIDTH = ATT_HEADS * HEAD_DIM
KV_WIDTH = ATT_KV_HEADS * HEAD_DIM
WINDOW = 128
RWKV_HEADS = 8
RWKV_WIDTH = RWKV_HEADS * HEAD_DIM
DECAY_LORA = 64
AAA_LORA = 64
GATE_LORA = 128
ATT_IN = ATT_WIDTH + 2 * KV_WIDTH
CONV_WIDTH = 3
RMS_EPS = 1e-5
LNX_EPS = 64e-5
NEG_INF = -1e30

CHUNK = 64
INV_BASE = 8
SUBLANES = 8
VMEM_LIMIT = 56 * 1024 * 1024


def _dot(a, b):
    return jnp.dot(a.astype(BF16), b.astype(BF16), preferred_element_type=F32)


def _dot_hi(a, b):
    return jnp.dot(a, b, preferred_element_type=F32, precision=HIGHEST)


def _dot_hi_nt(a, b):
    return lax.dot_general(a, b, (((1,), (1,)), ((), ())),
                           preferred_element_type=F32, precision=HIGHEST)


def _dot_hi_tn(a, b):
    return lax.dot_general(a, b, (((0,), (0,)), ((), ())),
                           preferred_element_type=F32, precision=HIGHEST)


def _rms_norm(x, g):
    ms = jnp.mean(x * x, axis=-1, keepdims=True)
    return x * lax.rsqrt(ms + RMS_EPS) * g


def _in_proj_kernel(x_ref, g_ref, w_ref, q_ref, kv_ref, rest_ref):
    h = _rms_norm(x_ref[...], g_ref[...]).astype(BF16)
    q_ref[...] = jnp.dot(h, w_ref[:, :ATT_WIDTH], preferred_element_type=F32)
    kv_ref[...] = jnp.dot(h, w_ref[:, ATT_WIDTH:ATT_IN], preferred_element_type=F32)
    rest_ref[...] = jnp.dot(h, w_ref[:, ATT_IN:], preferred_element_type=F32)


def _in_proj(xf, g, w, tm):
    n, d = xf.shape
    in_width = w.shape[1]
    shift_width = in_width - ATT_IN
    const = lambda i: (0, 0)
    row = lambda i: (i, 0)
    return pl.pallas_call(
        _in_proj_kernel,
        grid=(n // tm,),
        in_specs=[pl.BlockSpec((tm, d), row),
                  pl.BlockSpec((1, d), const),
                  pl.BlockSpec((d, in_width), const)],
        out_specs=[pl.BlockSpec((tm, ATT_WIDTH), row),
                   pl.BlockSpec((tm, 2 * KV_WIDTH), row),
                   pl.BlockSpec((tm, shift_width), row)],
        out_shape=[jax.ShapeDtypeStruct((n, ATT_WIDTH), F32),
                   jax.ShapeDtypeStruct((n, 2 * KV_WIDTH), F32),
                   jax.ShapeDtypeStruct((n, shift_width), F32)],
        compiler_params=pltpu.CompilerParams(
            dimension_semantics=("arbitrary",), vmem_limit_bytes=VMEM_LIMIT),
        name="in_proj",
    )(xf, g.reshape(1, d), w)


def _attn_kernel(sink_ref, q_ref, kvc_ref, kvp_ref, o_ref):
    blk = pl.program_id(1)
    q = q_ref[...] * (HEAD_DIM ** -0.5)
    kvc = kvc_ref[...]
    kvp = kvp_ref[...]
    qpos = lax.broadcasted_iota(jnp.int32, (WINDOW, 2 * WINDOW), 0)
    kpos = lax.broadcasted_iota(jnp.int32, (WINDOW, 2 * WINDOW), 1)
    dist = qpos + WINDOW - kpos
    valid = (dist >= 0) & (dist < WINDOW) & ((kpos >= WINDOW) | (blk > 0))
    distf = dist.astype(F32)
    outs = []
    for j in range(ATT_KV_HEADS):
        ks = slice(j * HEAD_DIM, (j + 1) * HEAD_DIM)
        vs = slice(KV_WIDTH + j * HEAD_DIM, KV_WIDTH + (j + 1) * HEAD_DIM)
        k = jnp.concatenate([kvp[:, ks], kvc[:, ks]], axis=0).astype(BF16)
        v = jnp.concatenate([kvp[:, vs], kvc[:, vs]], axis=0).astype(BF16)
        for g in range(ATT_GROUP):
            h = j * ATT_GROUP + g
            qh = q[:, h * HEAD_DIM:(h + 1) * HEAD_DIM].astype(BF16)
            s = lax.dot_general(qh, k, (((1,), (1,)), ((), ())), preferred_element_type=F32)
            s = s - (2.0 ** (-8.0 * (h + 1) / ATT_HEADS)) * distf
            s = jnp.where(valid, s, NEG_INF)
            sink = sink_ref[h]
            m = jnp.maximum(jnp.max(s, axis=-1, keepdims=True), sink)
            p = jnp.exp(s - m)
            denom = jnp.sum(p, axis=-1, keepdims=True) + jnp.exp(sink - m)
            outs.append(_dot(p / denom, v))
    o_ref[...] = jnp.concatenate(outs, axis=1)


def _attention(q, kv, sinks, batch, seq):
    nb = seq // WINDOW
    cur = lambda b, i: (b * nb + i, 0)
    prev = lambda b, i: (b * nb + jnp.maximum(i - 1, 0), 0)
    return pl.pallas_call(
        _attn_kernel,
        grid=(batch, nb),
        in_specs=[pl.BlockSpec(memory_space=pltpu.SMEM),
                  pl.BlockSpec((WINDOW, ATT_WIDTH), cur),
                  pl.BlockSpec((WINDOW, 2 * KV_WIDTH), cur),
                  pl.BlockSpec((WINDOW, 2 * KV_WIDTH), prev)],
        out_specs=pl.BlockSpec((WINDOW, ATT_WIDTH), cur),
        out_shape=jax.ShapeDtypeStruct((batch * seq, ATT_WIDTH), F32),
        compiler_params=pltpu.CompilerParams(
            dimension_semantics=("arbitrary", "arbitrary"), vmem_limit_bytes=VMEM_LIMIT),
        name="attn",
    )(sinks, q, kv, kv)


def _unit_lower_inverse(neg_a, row, col):
    eye = (row == col).astype(F32)
    d = jnp.where(row // INV_BASE == col // INV_BASE, neg_a, 0.0)
    t = eye + d
    span = 2
    while span < INV_BASE:
        d = _dot_hi(d, d)
        t = t + _dot_hi(t, d)
        span *= 2
    size = INV_BASE
    while size < CHUNK:
        off = jnp.where((row // (2 * size) == col // (2 * size)) & (row // size != col // size),
                        neg_a, 0.0)
        t = t + _dot_hi(_dot_hi(t, off), t)
        size *= 2
    return t


def _softplus(z):
    return jnp.maximum(z, 0.0) + jnp.log(1.0 + jnp.exp(-jnp.abs(z)))


def _rwkv_kernel(has_vmix, *refs):
    if has_vmix:
        (rest_ref, vfirst_ref, mu_ref, w0_ref, w2_ref, a0_ref, a2_ref, g2_ref, kk_ref, ka_ref,
         rk_ref, lng_ref, lnb_ref, v0_ref, v1_ref, v2_ref, out_ref, state_ref, halo_ref) = refs
    else:
        (rest_ref, mu_ref, w0_ref, w2_ref, a0_ref, a2_ref, g2_ref, kk_ref, ka_ref,
         rk_ref, lng_ref, lnb_ref, out_ref, vout_ref, state_ref, halo_ref) = refs
    C, W, N = CHUNK, RWKV_WIDTH, HEAD_DIM

    @pl.when(pl.program_id(1) == 0)
    def _():
        state_ref[...] = jnp.zeros_like(state_ref)
        halo_ref[...] = jnp.zeros_like(halo_ref)

    rest = rest_ref[...]
    trow = lax.broadcasted_iota(jnp.int32, rest.shape, 0)
    prev = jnp.where(trow == 0, halo_ref[SUBLANES - 1:SUBLANES, :], pltpu.roll(rest, 1, 0))
    halo_ref[...] = rest[C - SUBLANES:, :]
    xs = rest + (prev - rest) * mu_ref[...]
    r = xs[:, 0:W]
    k = xs[:, W:2 * W]
    v = xs[:, 2 * W:3 * W]
    wd = xs[:, 3 * W:3 * W + DECAY_LORA]
    ad = xs[:, 3 * W + DECAY_LORA:3 * W + DECAY_LORA + AAA_LORA]
    gd = xs[:, 3 * W + DECAY_LORA + AAA_LORA:]

    w = -_softplus(-(w0_ref[...] + _dot_hi(jnp.tanh(wd), w2_ref[...]))) - 0.5
    logd = -jnp.exp(w)
    a = jax.nn.sigmoid(a0_ref[...] + _dot_hi(ad, a2_ref[...]))
    g = _dot_hi(jax.nn.sigmoid(gd), g2_ref[...])

    hrow = lax.broadcasted_iota(jnp.int32, (W, W), 0)
    hcol = lax.broadcasted_iota(jnp.int32, (W, W), 1)
    head_sum = (hrow // N == hcol // N).astype(F32)

    kk = k * kk_ref[...]
    kk = kk / jnp.maximum(jnp.sqrt(_dot_hi(kk * kk, head_sum)), 1e-12)
    k = k * (1.0 + (a - 1.0) * ka_ref[...])
    if has_vmix:
        gate = jax.nn.sigmoid(v0_ref[...] + _dot_hi(_dot_hi(v, v1_ref[...]), v2_ref[...]))
        v = v + (vfirst_ref[...] - v) * gate
    else:
        vout_ref[...] = v

    row = lax.broadcasted_iota(jnp.int32, (C, C), 0)
    col = lax.broadcasted_iota(jnp.int32, (C, C), 1)
    strict = row > col
    incl = row >= col
    cum = _dot_hi(incl.astype(F32), logd)
    grow = jnp.exp(-cum)
    r_t = r * jnp.exp(cum)
    a_t = -kk * jnp.exp(cum - logd)
    k_t = k * grow
    b_t = kk * a * grow
    to_end = jnp.exp(cum[C - 1:C, :] - cum)
    k_e = k * to_end
    b_e = kk * a * to_end
    ones = jnp.ones((C, N), F32)

    ys = []
    for h in range(RWKV_HEADS):
        sl = slice(h * N, (h + 1) * N)
        vh = v[:, sl]
        st = state_ref[h]
        amat = _dot_hi_nt(jnp.concatenate([a_t[:, sl], r_t[:, sl]], axis=0),
                          jnp.concatenate([b_t[:, sl], k_t[:, sl]], axis=0))
        a_ab = jnp.where(strict, amat[:C, :C], 0.0)
        a_ak = jnp.where(strict, amat[:C, C:], 0.0)
        a_rb = jnp.where(incl, amat[C:, :C], 0.0)
        a_rk = jnp.where(incl, amat[C:, C:], 0.0)
        tinv = _unit_lower_inverse(a_ab, row, col)
        u = _dot_hi(tinv, _dot_hi(a_t[:, sl], st) + _dot_hi(a_ak, vh))
        ys.append(_dot_hi(r_t[:, sl], st) + _dot_hi(a_rb, u) + _dot_hi(a_rk, vh))
        decay_end = jnp.exp(_dot_hi_tn(logd[:, sl], ones))
        state_ref[h] = decay_end * st + _dot_hi_tn(b_e[:, sl], u) + _dot_hi_tn(k_e[:, sl], vh)
    y = jnp.concatenate(ys, axis=1)

    head_mean = head_sum * (1.0 / N)
    mean = _dot_hi(y, head_mean)
    yc = y - mean
    var = _dot_hi(yc * yc, head_mean)
    y = yc * lax.rsqrt(var + LNX_EPS) * lng_ref[...] + lnb_ref[...]
    bonus = _dot_hi(r * k * rk_ref[...], head_sum) * v
    out_ref[...] = (y + bonus) * g


def _rwkv(rest, vfirst, p, batch, seq):
    n, shift_width = rest.shape
    nc = seq // CHUNK
    W = RWKV_WIDTH
    tok = lambda b, c: (b * nc + c, 0)
    const = lambda b, c: (0, 0)
    vec = lambda a: a.reshape(1, -1)
    has_vmix = vfirst is not None
    full = lambda a: pl.BlockSpec(a.shape, const)

    args = [rest]
    specs = [pl.BlockSpec((CHUNK, shift_width), tok)]
    if has_vmix:
        args.append(vfirst)
        specs.append(pl.BlockSpec((CHUNK, W), tok))
    params = [vec(p["shift_mu"]), vec(p["w0"]), p["w2"], vec(p["a0"]), p["a2"], p["g2"],
              vec(p["k_k"]), vec(p["k_a"]), vec(p["r_k"]), vec(p["lnx_g"]), vec(p["lnx_b"])]
    if has_vmix:
        params += [vec(p["v0"]), p["v1"], p["v2"]]
    args += params
    specs += [full(a) for a in params]

    out_spec = pl.BlockSpec((CHUNK, W), tok)
    out_sds = jax.ShapeDtypeStruct((n, W), F32)
    outs = pl.pallas_call(
        functools.partial(_rwkv_kernel, has_vmix),
        grid=(batch, nc),
        in_specs=specs,
        out_specs=out_spec if has_vmix else [out_spec, out_spec],
        out_shape=out_sds if has_vmix else [out_sds, out_sds],
        scratch_shapes=[pltpu.VMEM((RWKV_HEADS, HEAD_DIM, HEAD_DIM), F32),
                        pltpu.VMEM((SUBLANES, shift_width), F32)],
        compiler_params=pltpu.CompilerParams(
            dimension_semantics=("arbitrary", "arbitrary"), vmem_limit_bytes=VMEM_LIMIT),
        name="rwkv_vmix" if has_vmix else "rwkv",
    )(*args)
    return (outs, vfirst) if has_vmix else (outs[0], outs[1])


def _ffn_kernel(final, *refs):
    if final:
        (x_ref, att_ref, rw_ref, wo_ref, n2g_ref, wg_ref, wu_ref, cw_ref, cb_ref, wd_ref, fg_ref,
         out_ref, halo_ref) = refs
    else:
        (x_ref, att_ref, rw_ref, wo_ref, n2g_ref, wg_ref, wu_ref, cw_ref, cb_ref, wd_ref,
         out_ref, halo_ref) = refs
    tm = x_ref.shape[0]

    @pl.when(pl.program_id(1) == 0)
    def _():
        halo_ref[...] = jnp.zeros_like(halo_ref)

    x = (x_ref[...] + _dot(att_ref[...], wo_ref[:ATT_WIDTH, :])
         + _dot(rw_ref[...], wo_ref[ATT_WIDTH:, :]))
    h = _rms_norm(x, n2g_ref[...]).astype(BF16)
    u = jnp.dot(h, wg_ref[...], preferred_element_type=F32)
    up = jnp.dot(h, wu_ref[...], preferred_element_type=F32)
    ext = jnp.concatenate([halo_ref[...], u], axis=0)
    halo_ref[...] = u[tm - SUBLANES:, :]
    c = (cb_ref[...] + cw_ref[0:1, :] * ext[SUBLANES - 2:SUBLANES - 2 + tm, :]
         + cw_ref[1:2, :] * ext[SUBLANES - 1:SUBLANES - 1 + tm, :] + cw_ref[2:3, :] * u)
    act = (c * jax.nn.sigmoid(c)) * up
    x = x + _dot(act, wd_ref[...])
    if final:
        x = _rms_norm(x, fg_ref[...])
    out_ref[...] = x


def _ffn(xf, att, rw, wo, n2g, wg, wu, cw, cb, wd, fg, batch, seq, tm):
    n, d = xf.shape
    ff = wg.shape[1]
    nt = seq // tm
    tok = lambda b, t: (b * nt + t, 0)
    const = lambda b, t: (0, 0)
    resident = lambda shape: pl.BlockSpec(shape, const, pipeline_mode=pl.Buffered(1))
    final = fg is not None
    args = [xf, att, rw, wo, n2g.reshape(1, d), wg, wu, cw, cb.reshape(1, ff), wd]
    specs = [pl.BlockSpec((tm, d), tok),
             pl.BlockSpec((tm, ATT_WIDTH), tok),
             pl.BlockSpec((tm, RWKV_WIDTH), tok),
             resident(wo.shape), resident((1, d)), resident(wg.shape), resident(wu.shape),
             resident(cw.shape), resident((1, ff)), resident(wd.shape)]
    if final:
        args.append(fg.reshape(1, d))
        specs.append(resident((1, d)))
    return pl.pallas_call(
        functools.partial(_ffn_kernel, final),
        grid=(batch, nt),
        in_specs=specs,
        out_specs=pl.BlockSpec((tm, d), tok),
        out_shape=jax.ShapeDtypeStruct((n, d), F32),
        scratch_shapes=[pltpu.VMEM((SUBLANES, ff), F32)],
        compiler_params=pltpu.CompilerParams(
            dimension_semantics=("arbitrary", "arbitrary"), vmem_limit_bytes=VMEM_LIMIT),
        name="ffn_final" if final else "ffn",
    )(*args)


def kernel(x, norm1_g, w_in, attn_sinks, shift_mu, w0, w2, a0, a2, g2, k_k, k_a, r_k, lnx_g, lnx_b,
           v0, v1, v2, w_out, norm2_g, ffn_w_gate, ffn_w_up, conv_w, conv_b, ffn_w_down, final_g):
    batch, seq, d = x.shape
    depth = w_in.shape[0]
    assert seq % WINDOW == 0 and seq % CHUNK == 0
    tm_proj = min(512, seq)
    tm_ffn = min(256, seq)
    xf = x.reshape(batch * seq, d)
    vfirst = None
    for l in range(depth):
        q, kv, rest = _in_proj(xf, norm1_g[l], w_in[l].astype(BF16), tm_proj)
        att = _attention(q, kv, attn_sinks[l], batch, seq)
        p = dict(shift_mu=shift_mu[l], w0=w0[l], w2=w2[l], a0=a0[l], a2=a2[l], g2=g2[l],
                 k_k=k_k[l], k_a=k_a[l], r_k=r_k[l], lnx_g=lnx_g[l], lnx_b=lnx_b[l])
        if l > 0:
            p.update(v0=v0[l - 1], v1=v1[l - 1], v2=v2[l - 1])
        rw, vfirst = _rwkv(rest, vfirst, p, batch, seq)
        xf = _ffn(xf, att, rw, w_out[l].astype(BF16), norm2_g[l], ffn_w_gate[l].astype(BF16),
                  ffn_w_up[l].astype(BF16), conv_w[l], conv_b[l], ffn_w_down[l].astype(BF16),
                  final_g if l == depth - 1 else None, batch, seq, tm_ffn)
    return xf.reshape(batch, seq, d)
```

```python
import functools

import jax
import jax.numpy as jnp
from jax import lax
from jax.experimental import pallas as pl
from jax.experimental.pallas import tpu as pltpu

F32 = jnp.float32
BF16 = jnp.bfloat16

HEAD_DIM = 64
ATT_HEADS = 8
ATT_KV_HEADS = 2
ATT_GROUP = ATT_HEADS // ATT_KV_HEADS
ATT_WIDTH = ATT_HEADS * HEAD_DIM
KV_WIDTH = ATT_KV_HEADS * HEAD_DIM
WINDOW = 128
RWKV_HEADS = 8
RWKV_WIDTH = RWKV_HEADS * HEAD_DIM
DECAY_LORA = 64
AAA_LORA = 64
GATE_LORA = 128
ATT_IN = ATT_WIDTH + 2 * KV_WIDTH
CONV_WIDTH = 3
RMS_EPS = 1e-5
LNX_EPS = 64e-5
NEG_INF = -1e30

CHUNK = 64
INV_BASE = 8
SUBLANES = 8
RWKV_ROWS = 2
CHUNK_DT = BF16
PREC_A = 1
PREC_INV = 1
PREC_APPLY = 1
PREC_SUM = 2
VMEM_LIMIT = 56 * 1024 * 1024


def _dot(a, b):
    return jnp.dot(a.astype(BF16), b.astype(BF16), preferred_element_type=F32)


def _rms_norm(x, g):
    ms = jnp.mean(x * x, axis=-1, keepdims=True)
    return x * lax.rsqrt(ms + RMS_EPS) * g


def _in_proj_kernel(x_ref, g_ref, w_ref, q_ref, kv_ref, rest_ref):
    h = _rms_norm(x_ref[...], g_ref[...]).astype(BF16)
    q_ref[...] = jnp.dot(h, w_ref[:, :ATT_WIDTH], preferred_element_type=F32)
    kv_ref[...] = jnp.dot(h, w_ref[:, ATT_WIDTH:ATT_IN], preferred_element_type=F32)
    rest_ref[...] = jnp.dot(h, w_ref[:, ATT_IN:], preferred_element_type=F32)


def _in_proj(xf, g, w, tm):
    n, d = xf.shape
    in_width = w.shape[1]
    shift_width = in_width - ATT_IN
    const = lambda i: (0, 0)
    row = lambda i: (i, 0)
    return pl.pallas_call(
        _in_proj_kernel,
        grid=(n // tm,),
        in_specs=[pl.BlockSpec((tm, d), row),
                  pl.BlockSpec((1, d), const),
                  pl.BlockSpec((d, in_width), const)],
        out_specs=[pl.BlockSpec((tm, ATT_WIDTH), row),
                   pl.BlockSpec((tm, 2 * KV_WIDTH), row),
                   pl.BlockSpec((tm, shift_width), row)],
        out_shape=[jax.ShapeDtypeStruct((n, ATT_WIDTH), F32),
                   jax.ShapeDtypeStruct((n, 2 * KV_WIDTH), F32),
                   jax.ShapeDtypeStruct((n, shift_width), F32)],
        compiler_params=pltpu.CompilerParams(
            dimension_semantics=("arbitrary",), vmem_limit_bytes=VMEM_LIMIT),
        name="in_proj",
    )(xf, g.reshape(1, d), w)


def _attn_kernel(sink_ref, q_ref, kvc_ref, kvp_ref, o_ref):
    blk = pl.program_id(1)
    q = q_ref[...] * (HEAD_DIM ** -0.5)
    kvc = kvc_ref[...]
    kvp = kvp_ref[...]
    qpos = lax.broadcasted_iota(jnp.int32, (WINDOW, 2 * WINDOW), 0)
    kpos = lax.broadcasted_iota(jnp.int32, (WINDOW, 2 * WINDOW), 1)
    dist = qpos + WINDOW - kpos
    valid = (dist >= 0) & (dist < WINDOW) & ((kpos >= WINDOW) | (blk > 0))
    distf = dist.astype(F32)
    outs = []
    for j in range(ATT_KV_HEADS):
        ks = slice(j * HEAD_DIM, (j + 1) * HEAD_DIM)
        vs = slice(KV_WIDTH + j * HEAD_DIM, KV_WIDTH + (j + 1) * HEAD_DIM)
        k = jnp.concatenate([kvp[:, ks], kvc[:, ks]], axis=0).astype(BF16)
        v = jnp.concatenate([kvp[:, vs], kvc[:, vs]], axis=0).astype(BF16)
        for g in range(ATT_GROUP):
            h = j * ATT_GROUP + g
            qh = q[:, h * HEAD_DIM:(h + 1) * HEAD_DIM].astype(BF16)
            s = lax.dot_general(qh, k, (((1,), (1,)), ((), ())), preferred_element_type=F32)
            s = s - (2.0 ** (-8.0 * (h + 1) / ATT_HEADS)) * distf
            s = jnp.where(valid, s, NEG_INF)
            sink = sink_ref[h]
            m = jnp.maximum(jnp.max(s, axis=-1, keepdims=True), sink)
            p = jnp.exp(s - m)
            denom = jnp.sum(p, axis=-1, keepdims=True) + jnp.exp(sink - m)
            outs.append(_dot(p / denom, v))
    o_ref[...] = jnp.concatenate(outs, axis=1)


def _attention(q, kv, sinks, batch, seq):
    nb = seq // WINDOW
    cur = lambda b, i: (b * nb + i, 0)
    prev = lambda b, i: (b * nb + jnp.maximum(i - 1, 0), 0)
    return pl.pallas_call(
        _attn_kernel,
        grid=(batch, nb),
        in_specs=[pl.BlockSpec(memory_space=pltpu.SMEM),
                  pl.BlockSpec((WINDOW, ATT_WIDTH), cur),
                  pl.BlockSpec((WINDOW, 2 * KV_WIDTH), cur),
                  pl.BlockSpec((WINDOW, 2 * KV_WIDTH), prev)],
        out_specs=pl.BlockSpec((WINDOW, ATT_WIDTH), cur),
        out_shape=jax.ShapeDtypeStruct((batch * seq, ATT_WIDTH), F32),
        compiler_params=pltpu.CompilerParams(
            dimension_semantics=("arbitrary", "arbitrary"), vmem_limit_bytes=VMEM_LIMIT),
        name="attn",
    )(sinks, q, kv, kv)


def _split(x, n):
    parts, rem = [], x
    for i in range(n):
        p = rem.astype(CHUNK_DT)
        parts.append(p)
        if i + 1 < n:
            rem = rem - p.astype(F32)
    return parts


_DIMS = {"nn": (((1,), (0,)), ((), ())), "nt": (((1,), (1,)), ((), ())), "tn": (((0,), (0,)), ((), ()))}


def _pdot(ap, bp, dims="nn"):
    order = max(len(ap), len(bp))
    acc = None
    for i, a in enumerate(ap):
        for j, b in enumerate(bp):
            if i + j < order:
                t = lax.dot_general(a, b, _DIMS[dims], preferred_element_type=F32)
                acc = t if acc is None else acc + t
    return acc


def _mdot(a, b, prec, dims="nn"):
    return _pdot(_split(a, prec), _split(b, prec), dims)


def _unit_lower_inverse(neg_as, row, col, prec):
    eye = (row == col).astype(F32)
    ds = [jnp.where(row // INV_BASE == col // INV_BASE, na, 0.0) for na in neg_as]
    ts = [eye + d for d in ds]
    span = 2
    while span < INV_BASE:
        ds = [_mdot(d, d, prec) for d in ds]
        ts = [t + _mdot(t, d, prec) for t, d in zip(ts, ds)]
        span *= 2
    size = INV_BASE
    while size < CHUNK:
        mask = (row // (2 * size) == col // (2 * size)) & (row // size != col // size)
        offs = [_split(jnp.where(mask, na, 0.0), prec) for na in neg_as]
        tps = [_split(t, prec) for t in ts]
        tos = [_split(_pdot(tp, off), prec) for tp, off in zip(tps, offs)]
        ts = [t + _pdot(to, tp) for t, to, tp in zip(ts, tos, tps)]
        size *= 2
    return ts


def _softplus(z):
    return jnp.maximum(z, 0.0) + jnp.log(1.0 + jnp.exp(-jnp.abs(z)))


def _rwkv_kernel(has_vmix, *refs):
    if has_vmix:
        (rest_ref, vfirst_ref, hs_ref, mu_ref, w0_ref, w2_ref, a0_ref, a2_ref, g2_ref, kk_ref,
         ka_ref, rk_ref, lng_ref, lnb_ref, v0_ref, v1_ref, v2_ref,
         out_ref, state_ref, halo_ref) = refs
    else:
        (rest_ref, hs_ref, mu_ref, w0_ref, w2_ref, a0_ref, a2_ref, g2_ref, kk_ref,
         ka_ref, rk_ref, lng_ref, lnb_ref, out_ref, vout_ref, state_ref, halo_ref) = refs
    C, W, N = CHUNK, RWKV_WIDTH, HEAD_DIM
    R = rest_ref.shape[0]
    RC = R * C

    @pl.when(pl.program_id(1) == 0)
    def _():
        state_ref[...] = jnp.zeros_like(state_ref)
        halo_ref[...] = jnp.zeros_like(halo_ref)

    rest = rest_ref[...].reshape(RC, rest_ref.shape[2])
    trow = lax.broadcasted_iota(jnp.int32, rest.shape, 0)
    prev = pltpu.roll(rest, 1, 0)
    for b in range(R):
        prev = jnp.where(trow == b * C, halo_ref[b, SUBLANES - 1:SUBLANES, :], prev)
        halo_ref[b] = rest[(b + 1) * C - SUBLANES:(b + 1) * C, :]
    xs = rest + (prev - rest) * mu_ref[...]
    r = xs[:, 0:W]
    k = xs[:, W:2 * W]
    v = xs[:, 2 * W:3 * W]
    wd = xs[:, 3 * W:3 * W + DECAY_LORA]
    ad = xs[:, 3 * W + DECAY_LORA:3 * W + DECAY_LORA + AAA_LORA]
    gd = xs[:, 3 * W + DECAY_LORA + AAA_LORA:]

    w = -_softplus(-(w0_ref[...] + _dot(jnp.tanh(wd), w2_ref[...]))) - 0.5
    logd = -jnp.exp(w)
    a = jax.nn.sigmoid(a0_ref[...] + _dot(ad, a2_ref[...]))
    g = _dot(jax.nn.sigmoid(gd), g2_ref[...])

    head_sum = [hs_ref[...]]

    kk = k * kk_ref[...]
    kk = kk / jnp.maximum(jnp.sqrt(_pdot(_split(kk * kk, PREC_SUM), head_sum)), 1e-12)
    k = k * (1.0 + (a - 1.0) * ka_ref[...])
    if has_vmix:
        gate = jax.nn.sigmoid(v0_ref[...] + _dot(_dot(v, v1_ref[...]), v2_ref[...]))
        v = v + (vfirst_ref[...].reshape(RC, W) - v) * gate
    else:
        vout_ref[...] = v.reshape(R, C, W)

    row = lax.broadcasted_iota(jnp.int32, (C, C), 0)
    col = lax.broadcasted_iota(jnp.int32, (C, C), 1)
    strict = row > col
    incl = row >= col
    brow = lax.broadcasted_iota(jnp.int32, (RC, RC), 0)
    bcol = lax.broadcasted_iota(jnp.int32, (RC, RC), 1)
    tri = ((brow >= bcol) & (brow // C == bcol // C)).astype(CHUNK_DT)
    cum = _pdot([tri], _split(logd, 3))
    crow = lax.broadcasted_iota(jnp.int32, (RC, W), 0)
    cum_end = cum[C - 1:C, :]
    for b in range(1, R):
        cum_end = jnp.where(crow >= b * C, cum[(b + 1) * C - 1:(b + 1) * C, :], cum_end)
    grow = jnp.exp(-cum)
    to_end = jnp.exp(cum_end - cum)
    decay_end = jnp.exp(cum_end)
    kka = kk * a
    r_t = _split(r * jnp.exp(cum), PREC_A)
    a_t = _split(-kk * jnp.exp(cum - logd), PREC_A)
    k_t = _split(k * grow, PREC_A)
    b_t = _split(kka * grow, PREC_A)
    k_e = _split(k * to_end, PREC_APPLY)
    b_e = _split(kka * to_end, PREC_APPLY)
    v_p = _split(v, PREC_APPLY)
    a_tp = a_t[:PREC_APPLY]
    r_tp = r_t[:PREC_APPLY]

    chains = [(b, h) for b in range(R) for h in range(RWKV_HEADS)]

    def cut(parts, b, h):
        return [p[b * C:(b + 1) * C, h * N:(h + 1) * N] for p in parts]

    def stack(xp, yp, b, h):
        return [jnp.concatenate([x, y], axis=0) for x, y in zip(cut(xp, b, h), cut(yp, b, h))]

    amats = [_pdot(stack(a_t, r_t, b, h), stack(b_t, k_t, b, h), "nt") for b, h in chains]
    tinvs = _unit_lower_inverse([jnp.where(strict, m[:C, :C], 0.0) for m in amats], row, col, PREC_INV)
    tinvs = [_split(t, PREC_APPLY) for t in tinvs]
    a_aks = [_split(jnp.where(strict, m[:C, C:], 0.0), PREC_APPLY) for m in amats]
    a_rbs = [_split(jnp.where(incl, m[C:, :C], 0.0), PREC_APPLY) for m in amats]
    a_rks = [_split(jnp.where(incl, m[C:, C:], 0.0), PREC_APPLY) for m in amats]
    vhs = [cut(v_p, b, h) for b, h in chains]
    sts = [state_ref[b, h] for b, h in chains]
    carry = [_pdot(stack(a_tp, r_tp, b, h), _split(st, PREC_APPLY), "nt")
             for (b, h), st in zip(chains, sts)]
    intra = [_pdot(ak, vh) for ak, vh in zip(a_aks, vhs)]
    us = [_split(_pdot(ti, _split(cr[:C] + it, PREC_APPLY)), PREC_APPLY)
          for ti, cr, it in zip(tinvs, carry, intra)]
    yh = [cr[C:] + _pdot(rb, u) + _pdot(rk, vh)
          for cr, rb, u, rk, vh in zip(carry, a_rbs, us, a_rks, vhs)]
    for (b, h), st, u, vh in zip(chains, sts, us, vhs):
        uv = [jnp.concatenate([x, y], axis=0) for x, y in zip(u, vh)]
        state_ref[b, h] = (decay_end[b * C:b * C + 1, h * N:(h + 1) * N] * st
                           + _pdot(uv, stack(b_e, k_e, b, h), "tn"))
    y = jnp.concatenate([jnp.concatenate(yh[b * RWKV_HEADS:(b + 1) * RWKV_HEADS], axis=1)
                         for b in range(R)], axis=0)

    mean = _pdot(_split(y, PREC_SUM), head_sum) * (1.0 / N)
    yc = y - mean
    var = _pdot(_split(yc * yc, PREC_SUM), head_sum) * (1.0 / N)
    y = yc * lax.rsqrt(var + LNX_EPS) * lng_ref[...] + lnb_ref[...]
    bonus = _pdot(_split(r * k * rk_ref[...], PREC_SUM), head_sum) * v
    out_ref[...] = ((y + bonus) * g).reshape(R, C, W)


def _rwkv(rest, vfirst, p, batch, seq):
    n, shift_width = rest.shape
    nc = seq // CHUNK
    W = RWKV_WIDTH
    rows = RWKV_ROWS if batch % RWKV_ROWS == 0 else 1
    tok = lambda b, c: (b, c, 0)
    const = lambda b, c: (0, 0)
    vec = lambda a: a.reshape(1, -1)
    has_vmix = vfirst is not None
    full = lambda a: pl.BlockSpec(a.shape, const)
    hidx = jnp.arange(W) // HEAD_DIM
    head_sum = (hidx[:, None] == hidx[None, :]).astype(BF16)

    args = [rest.reshape(batch, seq, shift_width)]
    specs = [pl.BlockSpec((rows, CHUNK, shift_width), tok)]
    if has_vmix:
        args.append(vfirst.reshape(batch, seq, W))
        specs.append(pl.BlockSpec((rows, CHUNK, W), tok))
    params = [head_sum, vec(p["shift_mu"]), vec(p["w0"]), p["w2"], vec(p["a0"]), p["a2"], p["g2"],
              vec(p["k_k"]), vec(p["k_a"]), vec(p["r_k"]), vec(p["lnx_g"]), vec(p["lnx_b"])]
    if has_vmix:
        params += [vec(p["v0"]), p["v1"], p["v2"]]
    args += params
    specs += [full(a) for a in params]

    out_spec = pl.BlockSpec((rows, CHUNK, W), tok)
    out_sds = jax.ShapeDtypeStruct((batch, seq, W), F32)
    outs = pl.pallas_call(
        functools.partial(_rwkv_kernel, has_vmix),
        grid=(batch // rows, nc),
        in_specs=specs,
        out_specs=out_spec if has_vmix else [out_spec, out_spec],
        out_shape=out_sds if has_vmix else [out_sds, out_sds],
        scratch_shapes=[pltpu.VMEM((rows, RWKV_HEADS, HEAD_DIM, HEAD_DIM), F32),
                        pltpu.VMEM((rows, SUBLANES, shift_width), F32)],
        compiler_params=pltpu.CompilerParams(
            dimension_semantics=("arbitrary", "arbitrary"), vmem_limit_bytes=VMEM_LIMIT),
        name="rwkv_vmix" if has_vmix else "rwkv",
    )(*args)
    if has_vmix:
        return outs.reshape(n, W), vfirst
    return outs[0].reshape(n, W), outs[1].reshape(n, W)


def _ffn_kernel(final, *refs):
    if final:
        (x_ref, att_ref, rw_ref, wo_ref, n2g_ref, wg_ref, wu_ref, cw_ref, cb_ref, wd_ref, fg_ref,
         out_ref, halo_ref) = refs
    else:
        (x_ref, att_ref, rw_ref, wo_ref, n2g_ref, wg_ref, wu_ref, cw_ref, cb_ref, wd_ref,
         out_ref, halo_ref) = refs
    tm = x_ref.shape[0]

    @pl.when(pl.program_id(1) == 0)
    def _():
        halo_ref[...] = jnp.zeros_like(halo_ref)

    x = (x_ref[...] + _dot(att_ref[...], wo_ref[:ATT_WIDTH, :])
         + _dot(rw_ref[...], wo_ref[ATT_WIDTH:, :]))
    h = _rms_norm(x, n2g_ref[...]).astype(BF16)
    u = jnp.dot(h, wg_ref[...], preferred_element_type=F32)
    up = jnp.dot(h, wu_ref[...], preferred_element_type=F32)
    ext = jnp.concatenate([halo_ref[...], u], axis=0)
    halo_ref[...] = u[tm - SUBLANES:, :]
    c = (cb_ref[...] + cw_ref[0:1, :] * ext[SUBLANES - 2:SUBLANES - 2 + tm, :]
         + cw_ref[1:2, :] * ext[SUBLANES - 1:SUBLANES - 1 + tm, :] + cw_ref[2:3, :] * u)
    act = (c * jax.nn.sigmoid(c)) * up
    x = x + _dot(act, wd_ref[...])
    if final:
        x = _rms_norm(x, fg_ref[...])
    out_ref[...] = x


def _ffn(xf, att, rw, wo, n2g, wg, wu, cw, cb, wd, fg, batch, seq, tm):
    n, d = xf.shape
    ff = wg.shape[1]
    nt = seq // tm
    tok = lambda b, t: (b * nt + t, 0)
    const = lambda b, t: (0, 0)
    resident = lambda shape: pl.BlockSpec(shape, const, pipeline_mode=pl.Buffered(1))
    final = fg is not None
    args = [xf, att, rw, wo, n2g.reshape(1, d), wg, wu, cw, cb.reshape(1, ff), wd]
    specs = [pl.BlockSpec((tm, d), tok),
             pl.BlockSpec((tm, ATT_WIDTH), tok),
             pl.BlockSpec((tm, RWKV_WIDTH), tok),
             resident(wo.shape), resident((1, d)), resident(wg.shape), resident(wu.shape),
             resident(cw.shape), resident((1, ff)), resident(wd.shape)]
    if final:
        args.append(fg.reshape(1, d))
        specs.append(resident((1, d)))
    return pl.pallas_call(
        functools.partial(_ffn_kernel, final),
        grid=(batch, nt),
        in_specs=specs,
        out_specs=pl.BlockSpec((tm, d), tok),
        out_shape=jax.ShapeDtypeStruct((n, d), F32),
        scratch_shapes=[pltpu.VMEM((SUBLANES, ff), F32)],
        compiler_params=pltpu.CompilerParams(
            dimension_semantics=("arbitrary", "arbitrary"), vmem_limit_bytes=VMEM_LIMIT),
        name="ffn_final" if final else "ffn",
    )(*args)


def kernel(x, norm1_g, w_in, attn_sinks, shift_mu, w0, w2, a0, a2, g2, k_k, k_a, r_k, lnx_g, lnx_b,
           v0, v1, v2, w_out, norm2_g, ffn_w_gate, ffn_w_up, conv_w, conv_b, ffn_w_down, final_g):
    batch, seq, d = x.shape
    depth = w_in.shape[0]
    assert seq % WINDOW == 0 and seq % CHUNK == 0
    tm_proj = min(512, seq)
    tm_ffn = min(256, seq)
    xf = x.reshape(batch * seq, d)
    vfirst = None
    for l in range(depth):
        q, kv, rest = _in_proj(xf, norm1_g[l], w_in[l].astype(BF16), tm_proj)
        att = _attention(q, kv, attn_sinks[l], batch, seq)
        p = dict(shift_mu=shift_mu[l], w0=w0[l], w2=w2[l], a0=a0[l], a2=a2[l], g2=g2[l],
                 k_k=k_k[l], k_a=k_a[l], r_k=r_k[l], lnx_g=lnx_g[l], lnx_b=lnx_b[l])
        if l > 0:
            p.update(v0=v0[l - 1], v1=v1[l - 1], v2=v2[l - 1])
        rw, vfirst = _rwkv(rest, vfirst, p, batch, seq)
        xf = _ffn(xf, att, rw, w_out[l].astype(BF16), norm2_g[l], ffn_w_gate[l].astype(BF16),
                  ffn_w_up[l].astype(BF16), conv_w[l], conv_b[l], ffn_w_down[l].astype(BF16),
                  final_g if l == depth - 1 else None, batch, seq, tm_ffn)
    return xf.reshape(batch, seq, d)
```

```python
import functools

import jax
import jax.numpy as jnp
from jax import lax
from jax.experimental import pallas as pl
from jax.experimental.pallas import tpu as pltpu

F32 = jnp.float32
BF16 = jnp.bfloat16

HEAD_DIM = 64
ATT_HEADS = 8
ATT_KV_HEADS = 2
ATT_GROUP = ATT_HEADS // ATT_KV_HEADS
ATT_WIDTH = ATT_HEADS * HEAD_DIM
KV_WIDTH = ATT_KV_HEADS * HEAD_DIM
WINDOW = 128
RWKV_HEADS = 8
RWKV_WIDTH = RWKV_HEADS * HEAD_DIM
DECAY_LORA = 64
AAA_LORA = 64
GATE_LORA = 128
ATT_IN = ATT_WIDTH + 2 * KV_WIDTH
CONV_WIDTH = 3
RMS_EPS = 1e-5
LNX_EPS = 64e-5
NEG_INF = -1e30

CHUNK = 64
INV_BASE = 8
SUBLANES = 8
RWKV_ROWS = 4
CHUNK_DT = BF16
PREC_A = 1
PREC_INV = 1
PREC_APPLY = 1
PREC_KNORM = 2
PREC_SUM = 1
PREC_CUM = 2
GROUP = 4
VMEM_LIMIT = 56 * 1024 * 1024


def _dot(a, b):
    return jnp.dot(a.astype(BF16), b.astype(BF16), preferred_element_type=F32)


def _rms_norm(x, g):
    ms = jnp.mean(x * x, axis=-1, keepdims=True)
    return x * lax.rsqrt(ms + RMS_EPS) * g


def _in_proj_kernel(x_ref, g_ref, w_ref, q_ref, kv_ref, rest_ref):
    h = _rms_norm(x_ref[...], g_ref[...]).astype(BF16)
    q_ref[...] = jnp.dot(h, w_ref[:, :ATT_WIDTH], preferred_element_type=F32)
    kv_ref[...] = jnp.dot(h, w_ref[:, ATT_WIDTH:ATT_IN], preferred_element_type=F32)
    rest_ref[...] = jnp.dot(h, w_ref[:, ATT_IN:], preferred_element_type=F32)


def _in_proj(xf, g, w, tm):
    n, d = xf.shape
    in_width = w.shape[1]
    shift_width = in_width - ATT_IN
    const = lambda i: (0, 0)
    row = lambda i: (i, 0)
    return pl.pallas_call(
        _in_proj_kernel,
        grid=(n // tm,),
        in_specs=[pl.BlockSpec((tm, d), row),
                  pl.BlockSpec((1, d), const),
                  pl.BlockSpec((d, in_width), const)],
        out_specs=[pl.BlockSpec((tm, ATT_WIDTH), row),
                   pl.BlockSpec((tm, 2 * KV_WIDTH), row),
                   pl.BlockSpec((tm, shift_width), row)],
        out_shape=[jax.ShapeDtypeStruct((n, ATT_WIDTH), F32),
                   jax.ShapeDtypeStruct((n, 2 * KV_WIDTH), F32),
                   jax.ShapeDtypeStruct((n, shift_width), F32)],
        compiler_params=pltpu.CompilerParams(
            dimension_semantics=("arbitrary",), vmem_limit_bytes=VMEM_LIMIT),
        name="in_proj",
    )(xf, g.reshape(1, d), w)


def _attn_kernel(sink_ref, q_ref, kvc_ref, kvp_ref, o_ref):
    blk = pl.program_id(1)
    q = q_ref[...] * (HEAD_DIM ** -0.5)
    kvc = kvc_ref[...]
    kvp = kvp_ref[...]
    qpos = lax.broadcasted_iota(jnp.int32, (WINDOW, 2 * WINDOW), 0)
    kpos = lax.broadcasted_iota(jnp.int32, (WINDOW, 2 * WINDOW), 1)
    dist = qpos + WINDOW - kpos
    valid = (dist >= 0) & (dist < WINDOW) & ((kpos >= WINDOW) | (blk > 0))
    distf = dist.astype(F32)
    outs = []
    for j in range(ATT_KV_HEADS):
        ks = slice(j * HEAD_DIM, (j + 1) * HEAD_DIM)
        vs = slice(KV_WIDTH + j * HEAD_DIM, KV_WIDTH + (j + 1) * HEAD_DIM)
        k = jnp.concatenate([kvp[:, ks], kvc[:, ks]], axis=0).astype(BF16)
        v = jnp.concatenate([kvp[:, vs], kvc[:, vs]], axis=0).astype(BF16)
        for g in range(ATT_GROUP):
            h = j * ATT_GROUP + g
            qh = q[:, h * HEAD_DIM:(h + 1) * HEAD_DIM].astype(BF16)
            s = lax.dot_general(qh, k, (((1,), (1,)), ((), ())), preferred_element_type=F32)
            s = s - (2.0 ** (-8.0 * (h + 1) / ATT_HEADS)) * distf
            s = jnp.where(valid, s, NEG_INF)
            sink = sink_ref[h]
            m = jnp.maximum(jnp.max(s, axis=-1, keepdims=True), sink)
            p = jnp.exp(s - m)
            denom = jnp.sum(p, axis=-1, keepdims=True) + jnp.exp(sink - m)
            outs.append(_dot(p / denom, v))
    o_ref[...] = jnp.concatenate(outs, axis=1)


def _attention(q, kv, sinks, batch, seq):
    nb = seq // WINDOW
    cur = lambda b, i: (b * nb + i, 0)
    prev = lambda b, i: (b * nb + jnp.maximum(i - 1, 0), 0)
    return pl.pallas_call(
        _attn_kernel,
        grid=(batch, nb),
        in_specs=[pl.BlockSpec(memory_space=pltpu.SMEM),
                  pl.BlockSpec((WINDOW, ATT_WIDTH), cur),
                  pl.BlockSpec((WINDOW, 2 * KV_WIDTH), cur),
                  pl.BlockSpec((WINDOW, 2 * KV_WIDTH), prev)],
        out_specs=pl.BlockSpec((WINDOW, ATT_WIDTH), cur),
        out_shape=jax.ShapeDtypeStruct((batch * seq, ATT_WIDTH), F32),
        compiler_params=pltpu.CompilerParams(
            dimension_semantics=("arbitrary", "arbitrary"), vmem_limit_bytes=VMEM_LIMIT),
        name="attn",
    )(sinks, q, kv, kv)


def _split(x, n):
    parts, rem = [], x
    for i in range(n):
        p = rem.astype(CHUNK_DT)
        parts.append(p)
        if i + 1 < n:
            rem = rem - p.astype(F32)
    return parts


_DIMS = {"nn": (((1,), (0,)), ((), ())), "nt": (((1,), (1,)), ((), ())), "tn": (((0,), (0,)), ((), ()))}


def _pdot(ap, bp, dims="nn"):
    order = max(len(ap), len(bp))
    acc = None
    for i, a in enumerate(ap):
        for j, b in enumerate(bp):
            if i + j < order:
                t = lax.dot_general(a, b, _DIMS[dims], preferred_element_type=F32)
                acc = t if acc is None else acc + t
    return acc


def _block_diag(parts, bd_mask):
    return [jnp.where(bd_mask, jnp.concatenate([p] * GROUP, axis=0), jnp.zeros((), p.dtype)) for p in parts]


def _gdot(x, y, bd_mask, prec, dims="nn"):
    ybd = jnp.where(bd_mask, jnp.concatenate([y] * GROUP, axis=0), 0.0)
    return _pdot(_split(x, prec), _split(ybd, prec), dims)


def _unit_lower_inverse(neg_as, rowi, colh, bd_mask, prec):
    C = CHUNK
    eye = (rowi == colh).astype(F32)
    ds = [jnp.where(rowi // INV_BASE == colh // INV_BASE, na, 0.0) for na in neg_as]
    ts = [eye + d for d in ds]
    ds = [_gdot(d, d, bd_mask, prec) for d in ds]
    span = 4
    while span < INV_BASE:
        xs = [_gdot(jnp.concatenate([t, d], axis=0), d, bd_mask, prec) for t, d in zip(ts, ds)]
        ts = [t + x[:C] for t, x in zip(ts, xs)]
        ds = [x[C:] for x in xs]
        span *= 2
    ts = [t + _gdot(t, d, bd_mask, prec) for t, d in zip(ts, ds)]
    size = INV_BASE
    while size < C:
        mask = (rowi // (2 * size) == colh // (2 * size)) & (rowi // size != colh // size)
        tos = [_gdot(t, jnp.where(mask, na, 0.0), bd_mask, prec) for t, na in zip(ts, neg_as)]
        ts = [t + _gdot(to, t, bd_mask, prec) for t, to in zip(ts, tos)]
        size *= 2
    return ts


def _softplus(z):
    return jnp.maximum(z, 0.0) + jnp.log(1.0 + jnp.exp(-jnp.abs(z)))


def _rwkv_kernel(has_vmix, *refs):
    if has_vmix:
        (rest_ref, vfirst_ref, hs_ref, mu_ref, w0_ref, w2_ref, a0_ref, a2_ref, g2_ref, kk_ref,
         ka_ref, rk_ref, lng_ref, lnb_ref, v0_ref, v1_ref, v2_ref,
         out_ref, state_ref, halo_ref) = refs
    else:
        (rest_ref, hs_ref, mu_ref, w0_ref, w2_ref, a0_ref, a2_ref, g2_ref, kk_ref,
         ka_ref, rk_ref, lng_ref, lnb_ref, out_ref, vout_ref, state_ref, halo_ref) = refs
    C, W, N = CHUNK, RWKV_WIDTH, HEAD_DIM
    R = rest_ref.shape[0]
    RC = R * C

    @pl.when(pl.program_id(1) == 0)
    def _():
        state_ref[...] = jnp.zeros_like(state_ref)
        halo_ref[...] = jnp.zeros_like(halo_ref)

    rest = rest_ref[...].reshape(RC, rest_ref.shape[2])
    trow = lax.broadcasted_iota(jnp.int32, rest.shape, 0)
    prev = pltpu.roll(rest, 1, 0)
    for b in range(R):
        prev = jnp.where(trow == b * C, halo_ref[b, SUBLANES - 1:SUBLANES, :], prev)
        halo_ref[b] = rest[(b + 1) * C - SUBLANES:(b + 1) * C, :]
    xs = rest + (prev - rest) * mu_ref[...]
    r = xs[:, 0:W]
    k = xs[:, W:2 * W]
    v = xs[:, 2 * W:3 * W]
    wd = xs[:, 3 * W:3 * W + DECAY_LORA]
    ad = xs[:, 3 * W + DECAY_LORA:3 * W + DECAY_LORA + AAA_LORA]
    gd = xs[:, 3 * W + DECAY_LORA + AAA_LORA:]

    w = -_softplus(-(w0_ref[...] + _dot(jnp.tanh(wd), w2_ref[...]))) - 0.5
    logd = -jnp.exp(w)
    a = jax.nn.sigmoid(a0_ref[...] + _dot(ad, a2_ref[...]))
    g = _dot(jax.nn.sigmoid(gd), g2_ref[...])

    ones_bd = hs_ref[...]
    GW = ones_bd.shape[0]
    bd_mask = (lax.broadcasted_iota(jnp.int32, (GW, GW), 0) // N
               == lax.broadcasted_iota(jnp.int32, (GW, GW), 1) // N)

    def head_sum(x, prec):
        xs_ = jnp.concatenate([x[:, i * GW:(i + 1) * GW] for i in range(W // GW)], axis=0)
        s = _pdot(_split(xs_, prec), [ones_bd])
        return jnp.concatenate([s[i * RC:(i + 1) * RC] for i in range(W // GW)], axis=1)

    kk = k * kk_ref[...]
    kk = kk / jnp.maximum(jnp.sqrt(head_sum(kk * kk, PREC_KNORM)), 1e-12)
    k = k * (1.0 + (a - 1.0) * ka_ref[...])
    if has_vmix:
        gate = jax.nn.sigmoid(v0_ref[...] + _dot(_dot(v, v1_ref[...]), v2_ref[...]))
        v = v + (vfirst_ref[...].reshape(RC, W) - v) * gate
    else:
        vout_ref[...] = v.reshape(R, C, W)

    brow = lax.broadcasted_iota(jnp.int32, (RC, RC), 0)
    bcol = lax.broadcasted_iota(jnp.int32, (RC, RC), 1)
    tri = ((brow >= bcol) & (brow // C == bcol // C)).astype(CHUNK_DT)
    cum = _pdot([tri], _split(logd, PREC_CUM))
    crow = lax.broadcasted_iota(jnp.int32, (RC, W), 0)
    cum_end = cum[C - 1:C, :]
    for b in range(1, R):
        cum_end = jnp.where(crow >= b * C, cum[(b + 1) * C - 1:(b + 1) * C, :], cum_end)
    grow = jnp.exp(-cum)
    to_end = jnp.exp(cum_end - cum)
    decay_end = jnp.exp(cum_end)
    kka = kk * a
    r_t = r * jnp.exp(cum)
    a_t = -kk * jnp.exp(cum - logd)
    k_t = k * grow
    b_t = kka * grow
    k_e = k * to_end
    b_e = kka * to_end

    slabs = [(b, i) for b in range(R) for i in range(W // GW)]

    def cut(x, b, i):
        return x[b * C:(b + 1) * C, i * GW:(i + 1) * GW]

    rowi = lax.broadcasted_iota(jnp.int32, (C, GW), 0)
    colh = lax.broadcasted_iota(jnp.int32, (C, GW), 1) % N
    strict = rowi > colh
    incl = rowi >= colh
    ar = [jnp.concatenate([cut(a_t, b, i), cut(r_t, b, i)], axis=0) for b, i in slabs]
    am_b = [_gdot(x, cut(b_t, b, i), bd_mask, PREC_A, "nt") for x, (b, i) in zip(ar, slabs)]
    am_k = [_gdot(x, cut(k_t, b, i), bd_mask, PREC_A, "nt") for x, (b, i) in zip(ar, slabs)]
    tinvs = _unit_lower_inverse([jnp.where(strict, m[:C], 0.0) for m in am_b], rowi, colh, bd_mask, PREC_INV)
    a_ak_rk = [jnp.concatenate([jnp.where(strict, m[:C], 0.0), jnp.where(incl, m[C:], 0.0)], axis=0)
               for m in am_k]
    a_rb = [jnp.where(incl, m[C:], 0.0) for m in am_b]
    sts = [state_ref[b, i] for b, i in slabs]
    carry = [_gdot(x, st, bd_mask, PREC_APPLY, "nt") for x, st in zip(ar, sts)]
    intra = [_gdot(x, cut(v, b, i), bd_mask, PREC_APPLY) for x, (b, i) in zip(a_ak_rk, slabs)]
    us = [_gdot(ti, cr[:C] + it[:C], bd_mask, PREC_APPLY)
          for ti, cr, it in zip(tinvs, carry, intra)]
    ys = [cr[C:] + it[C:] + _gdot(rb, u, bd_mask, PREC_APPLY)
          for cr, it, rb, u in zip(carry, intra, a_rb, us)]
    for (b, i), st, u in zip(slabs, sts, us):
        uv = jnp.concatenate([u, cut(v, b, i)], axis=0)
        bk = jnp.concatenate([cut(b_e, b, i), cut(k_e, b, i)], axis=0)
        full = _pdot(_split(uv, PREC_APPLY), _split(bk, PREC_APPLY), "tn")
        full = jnp.where(bd_mask, full, 0.0)
        st = cut(decay_end, b, i)[0:1, :] * st
        for h in range(GROUP):
            st = st + full[h * N:(h + 1) * N, :]
        state_ref[b, i] = st
    y = jnp.concatenate([jnp.concatenate(ys[b * (W // GW):(b + 1) * (W // GW)], axis=1)
                         for b in range(R)], axis=0)

    mean = head_sum(y, PREC_SUM) * (1.0 / N)
    yc = y - mean
    var = head_sum(yc * yc, PREC_SUM) * (1.0 / N)
    y = yc * lax.rsqrt(var + LNX_EPS) * lng_ref[...] + lnb_ref[...]
    bonus = head_sum(r * k * rk_ref[...], PREC_SUM) * v
    out_ref[...] = ((y + bonus) * g).reshape(R, C, W)


def _rwkv(rest, vfirst, p, batch, seq):
    n, shift_width = rest.shape
    nc = seq // CHUNK
    W = RWKV_WIDTH
    rows = RWKV_ROWS if batch % RWKV_ROWS == 0 else 1
    tok = lambda b, c: (b, c, 0)
    const = lambda b, c: (0, 0)
    vec = lambda a: a.reshape(1, -1)
    has_vmix = vfirst is not None
    full = lambda a: pl.BlockSpec(a.shape, const)
    gw = GROUP * HEAD_DIM
    hidx = jnp.arange(gw) // HEAD_DIM
    bd_mask = (hidx[:, None] == hidx[None, :]).astype(BF16)

    args = [rest.reshape(batch, seq, shift_width)]
    specs = [pl.BlockSpec((rows, CHUNK, shift_width), tok)]
    if has_vmix:
        args.append(vfirst.reshape(batch, seq, W))
        specs.append(pl.BlockSpec((rows, CHUNK, W), tok))
    params = [bd_mask, vec(p["shift_mu"]), vec(p["w0"]), p["w2"], vec(p["a0"]), p["a2"], p["g2"],
              vec(p["k_k"]), vec(p["k_a"]), vec(p["r_k"]), vec(p["lnx_g"]), vec(p["lnx_b"])]
    if has_vmix:
        params += [vec(p["v0"]), p["v1"], p["v2"]]
    args += params
    specs += [full(a) for a in params]

    out_spec = pl.BlockSpec((rows, CHUNK, W), tok)
    out_sds = jax.ShapeDtypeStruct((batch, seq, W), F32)
    outs = pl.pallas_call(
        functools.partial(_rwkv_kernel, has_vmix),
        grid=(batch // rows, nc),
        in_specs=specs,
        out_specs=out_spec if has_vmix else [out_spec, out_spec],
        out_shape=out_sds if has_vmix else [out_sds, out_sds],
        scratch_shapes=[pltpu.VMEM((rows, W // gw, HEAD_DIM, gw), F32),
                        pltpu.VMEM((rows, SUBLANES, shift_width), F32)],
        compiler_params=pltpu.CompilerParams(
            dimension_semantics=("arbitrary", "arbitrary"), vmem_limit_bytes=VMEM_LIMIT),
        name="rwkv_vmix" if has_vmix else "rwkv",
    )(*args)
    if has_vmix:
        return outs.reshape(n, W), vfirst
    return outs[0].reshape(n, W), outs[1].reshape(n, W)


def _ffn_kernel(final, *refs):
    if final:
        (x_ref, att_ref, rw_ref, wo_ref, n2g_ref, wg_ref, wu_ref, cw_ref, cb_ref, wd_ref, fg_ref,
         out_ref, halo_ref) = refs
    else:
        (x_ref, att_ref, rw_ref, wo_ref, n2g_ref, wg_ref, wu_ref, cw_ref, cb_ref, wd_ref,
         out_ref, halo_ref) = refs
    tm = x_ref.shape[0]

    @pl.when(pl.program_id(1) == 0)
    def _():
        halo_ref[...] = jnp.zeros_like(halo_ref)

    x = (x_ref[...] + _dot(att_ref[...], wo_ref[:ATT_WIDTH, :])
         + _dot(rw_ref[...], wo_ref[ATT_WIDTH:, :]))
    h = _rms_norm(x, n2g_ref[...]).astype(BF16)
    u = jnp.dot(h, wg_ref[...], preferred_element_type=F32)
    up = jnp.dot(h, wu_ref[...], preferred_element_type=F32)
    ext = jnp.concatenate([halo_ref[...], u], axis=0)
    halo_ref[...] = u[tm - SUBLANES:, :]
    c = (cb_ref[...] + cw_ref[0:1, :] * ext[SUBLANES - 2:SUBLANES - 2 + tm, :]
         + cw_ref[1:2, :] * ext[SUBLANES - 1:SUBLANES - 1 + tm, :] + cw_ref[2:3, :] * u)
    act = (c * jax.nn.sigmoid(c)) * up
    x = x + _dot(act, wd_ref[...])
    if final:
        x = _rms_norm(x, fg_ref[...])
    out_ref[...] = x


def _ffn(xf, att, rw, wo, n2g, wg, wu, cw, cb, wd, fg, batch, seq, tm):
    n, d = xf.shape
    ff = wg.shape[1]
    nt = seq // tm
    tok = lambda b, t: (b * nt + t, 0)
    const = lambda b, t: (0, 0)
    resident = lambda shape: pl.BlockSpec(shape, const, pipeline_mode=pl.Buffered(1))
    final = fg is not None
    args = [xf, att, rw, wo, n2g.reshape(1, d), wg, wu, cw, cb.reshape(1, ff), wd]
    specs = [pl.BlockSpec((tm, d), tok),
             pl.BlockSpec((tm, ATT_WIDTH), tok),
             pl.BlockSpec((tm, RWKV_WIDTH), tok),
             resident(wo.shape), resident((1, d)), resident(wg.shape), resident(wu.shape),
             resident(cw.shape), resident((1, ff)), resident(wd.shape)]
    if final:
        args.append(fg.reshape(1, d))
        specs.append(resident((1, d)))
    return pl.pallas_call(
        functools.partial(_ffn_kernel, final),
        grid=(batch, nt),
        in_specs=specs,
        out_specs=pl.BlockSpec((tm, d), tok),
        out_shape=jax.ShapeDtypeStruct((n, d), F32),
        scratch_shapes=[pltpu.VMEM((SUBLANES, ff), F32)],
        compiler_params=pltpu.CompilerParams(
            dimension_semantics=("arbitrary", "arbitrary"), vmem_limit_bytes=VMEM_LIMIT),
        name="ffn_final" if final else "ffn",
    )(*args)


def kernel(x, norm1_g, w_in, attn_sinks, shift_mu, w0, w2, a0, a2, g2, k_k, k_a, r_k, lnx_g, lnx_b,
           v0, v1, v2, w_out, norm2_g, ffn_w_gate, ffn_w_up, conv_w, conv_b, ffn_w_down, final_g):
    batch, seq, d = x.shape
    depth = w_in.shape[0]
    assert seq % WINDOW == 0 and seq % CHUNK == 0
    tm_proj = min(512, seq)
    tm_ffn = min(256, seq)
    xf = x.reshape(batch * seq, d)
    vfirst = None
    for l in range(depth):
        q, kv, rest = _in_proj(xf, norm1_g[l], w_in[l].astype(BF16), tm_proj)
        att = _attention(q, kv, attn_sinks[l], batch, seq)
        p = dict(shift_mu=shift_mu[l], w0=w0[l], w2=w2[l], a0=a0[l], a2=a2[l], g2=g2[l],
                 k_k=k_k[l], k_a=k_a[l], r_k=r_k[l], lnx_g=lnx_g[l], lnx_b=lnx_b[l])
        if l > 0:
            p.update(v0=v0[l - 1], v1=v1[l - 1], v2=v2[l - 1])
        rw, vfirst = _rwkv(rest, vfirst, p, batch, seq)
        xf = _ffn(xf, att, rw, w_out[l].astype(BF16), norm2_g[l], ffn_w_gate[l].astype(BF16),
                  ffn_w_up[l].astype(BF16), conv_w[l], conv_b[l], ffn_w_down[l].astype(BF16),
                  final_g if l == depth - 1 else None, batch, seq, tm_ffn)
    return xf.reshape(batch, seq, d)
```

```python
import functools
import math

import jax
import jax.numpy as jnp
from jax import lax
from jax.experimental import pallas as pl
from jax.experimental.pallas import tpu as pltpu

F32 = jnp.float32
BF16 = jnp.bfloat16

HEAD_DIM = 64
ATT_HEADS = 8
ATT_KV_HEADS = 2
ATT_GROUP = ATT_HEADS // ATT_KV_HEADS
ATT_WIDTH = ATT_HEADS * HEAD_DIM
KV_WIDTH = ATT_KV_HEADS * HEAD_DIM
WINDOW = 128
RWKV_HEADS = 8
RWKV_WIDTH = RWKV_HEADS * HEAD_DIM
DECAY_LORA = 64
AAA_LORA = 64
GATE_LORA = 128
ATT_IN = ATT_WIDTH + 2 * KV_WIDTH
CONV_WIDTH = 3
RMS_EPS = 1e-5
LNX_EPS = 64e-5
DECAY_SCALE = math.exp(-0.5)
NEG_INF = -1e30

CHUNK = 64
INV_BASE = 8
SUBLANES = 8
RWKV_ROWS = 4
CHUNK_DT = BF16
PREC_A = 1
PREC_INV = 1
PREC_APPLY = 1
PREC_KNORM = 1
PREC_SUM = 1
PREC_CUM = 2
GROUP = 4
VMEM_LIMIT = 56 * 1024 * 1024


def _dot(a, b):
    return jnp.dot(a.astype(BF16), b.astype(BF16), preferred_element_type=F32)


def _rms_norm(x, g):
    ms = jnp.mean(x * x, axis=-1, keepdims=True)
    return x * lax.rsqrt(ms + RMS_EPS) * g


def _in_proj_kernel(tiles_per_seq, x_ref, g_ref, w_ref, mu_ref, q_ref, kv_ref, shifted_ref, halo_ref):
    tm = x_ref.shape[0]

    @pl.when(pl.program_id(0) % tiles_per_seq == 0)
    def _():
        halo_ref[...] = jnp.zeros_like(halo_ref)

    h = _rms_norm(x_ref[...], g_ref[...]).astype(BF16)
    q_ref[...] = jnp.dot(h, w_ref[:, :ATT_WIDTH], preferred_element_type=F32)
    kv_ref[...] = jnp.dot(h, w_ref[:, ATT_WIDTH:ATT_IN], preferred_element_type=F32)
    rest = jnp.dot(h, w_ref[:, ATT_IN:], preferred_element_type=F32)
    trow = lax.broadcasted_iota(jnp.int32, rest.shape, 0)
    prev = jnp.where(trow == 0, halo_ref[SUBLANES - 1:SUBLANES, :], pltpu.roll(rest, 1, 0))
    halo_ref[...] = rest[tm - SUBLANES:, :]
    shifted_ref[...] = rest + (prev - rest) * mu_ref[...]


def _in_proj(xf, g, w, mu, seq, tm):
    n, d = xf.shape
    in_width = w.shape[1]
    shift_width = in_width - ATT_IN
    const = lambda i: (0, 0)
    row = lambda i: (i, 0)
    return pl.pallas_call(
        functools.partial(_in_proj_kernel, seq // tm),
        grid=(n // tm,),
        in_specs=[pl.BlockSpec((tm, d), row),
                  pl.BlockSpec((1, d), const),
                  pl.BlockSpec((d, in_width), const),
                  pl.BlockSpec((1, shift_width), const)],
        out_specs=[pl.BlockSpec((tm, ATT_WIDTH), row),
                   pl.BlockSpec((tm, 2 * KV_WIDTH), row),
                   pl.BlockSpec((tm, shift_width), row)],
        out_shape=[jax.ShapeDtypeStruct((n, ATT_WIDTH), F32),
                   jax.ShapeDtypeStruct((n, 2 * KV_WIDTH), F32),
                   jax.ShapeDtypeStruct((n, shift_width), F32)],
        scratch_shapes=[pltpu.VMEM((SUBLANES, shift_width), F32)],
        compiler_params=pltpu.CompilerParams(
            dimension_semantics=("arbitrary",), vmem_limit_bytes=VMEM_LIMIT),
        name="in_proj",
    )(xf, g.reshape(1, d), w, mu.reshape(1, shift_width))


def _attn_kernel(sink_ref, q_ref, kvc_ref, kvp_ref, o_ref):
    blk = pl.program_id(1)
    q = q_ref[...] * (HEAD_DIM ** -0.5)
    kvc = kvc_ref[...]
    kvp = kvp_ref[...]
    qpos = lax.broadcasted_iota(jnp.int32, (WINDOW, 2 * WINDOW), 0)
    kpos = lax.broadcasted_iota(jnp.int32, (WINDOW, 2 * WINDOW), 1)
    dist = qpos + WINDOW - kpos
    valid = (dist >= 0) & (dist < WINDOW) & ((kpos >= WINDOW) | (blk > 0))
    distf = dist.astype(F32)
    outs = []
    for j in range(ATT_KV_HEADS):
        ks = slice(j * HEAD_DIM, (j + 1) * HEAD_DIM)
        vs = slice(KV_WIDTH + j * HEAD_DIM, KV_WIDTH + (j + 1) * HEAD_DIM)
        k = jnp.concatenate([kvp[:, ks], kvc[:, ks]], axis=0).astype(BF16)
        v = jnp.concatenate([kvp[:, vs], kvc[:, vs]], axis=0).astype(BF16)
        for g in range(ATT_GROUP):
            h = j * ATT_GROUP + g
            qh = q[:, h * HEAD_DIM:(h + 1) * HEAD_DIM].astype(BF16)
            s = lax.dot_general(qh, k, (((1,), (1,)), ((), ())), preferred_element_type=F32)
            s = s - (2.0 ** (-8.0 * (h + 1) / ATT_HEADS)) * distf
            s = jnp.where(valid, s, NEG_INF)
            sink = sink_ref[h]
            m = jnp.maximum(jnp.max(s, axis=-1, keepdims=True), sink)
            p = jnp.exp(s - m)
            denom = jnp.sum(p, axis=-1, keepdims=True) + jnp.exp(sink - m)
            outs.append(_dot(p / denom, v))
    o_ref[...] = jnp.concatenate(outs, axis=1)


def _attention(q, kv, sinks, batch, seq):
    nb = seq // WINDOW
    cur = lambda b, i: (b * nb + i, 0)
    prev = lambda b, i: (b * nb + jnp.maximum(i - 1, 0), 0)
    return pl.pallas_call(
        _attn_kernel,
        grid=(batch, nb),
        in_specs=[pl.BlockSpec(memory_space=pltpu.SMEM),
                  pl.BlockSpec((WINDOW, ATT_WIDTH), cur),
                  pl.BlockSpec((WINDOW, 2 * KV_WIDTH), cur),
                  pl.BlockSpec((WINDOW, 2 * KV_WIDTH), prev)],
        out_specs=pl.BlockSpec((WINDOW, ATT_WIDTH), cur),
        out_shape=jax.ShapeDtypeStruct((batch * seq, ATT_WIDTH), F32),
        compiler_params=pltpu.CompilerParams(
            dimension_semantics=("arbitrary", "arbitrary"), vmem_limit_bytes=VMEM_LIMIT),
        name="attn",
    )(sinks, q, kv, kv)


def _split(x, n):
    parts, rem = [], x
    for i in range(n):
        p = rem.astype(CHUNK_DT)
        parts.append(p)
        if i + 1 < n:
            rem = rem - p.astype(F32)
    return parts


_DIMS = {"nn": (((1,), (0,)), ((), ())), "nt": (((1,), (1,)), ((), ())), "tn": (((0,), (0,)), ((), ()))}


def _pdot(ap, bp, dims="nn"):
    order = max(len(ap), len(bp))
    acc = None
    for i, a in enumerate(ap):
        for j, b in enumerate(bp):
            if i + j < order:
                t = lax.dot_general(a, b, _DIMS[dims], preferred_element_type=F32)
                acc = t if acc is None else acc + t
    return acc


def _block_diag(parts, bd_mask):
    return [jnp.where(bd_mask, jnp.concatenate([p] * GROUP, axis=0), jnp.zeros((), p.dtype)) for p in parts]


def _gdot(x, y, bd_mask, prec, dims="nn"):
    ybd = jnp.where(bd_mask, jnp.concatenate([y] * GROUP, axis=0), 0.0)
    return _pdot(_split(x, prec), _split(ybd, prec), dims)


def _unit_lower_inverse(neg_as, rowi, colh, bd_mask, prec):
    C = CHUNK
    eye = (rowi == colh).astype(F32)
    ds = [jnp.where(rowi // INV_BASE == colh // INV_BASE, na, 0.0) for na in neg_as]
    ts = [eye + d for d in ds]
    ds = [_gdot(d, d, bd_mask, prec) for d in ds]
    span = 4
    while span < INV_BASE:
        xs = [_gdot(jnp.concatenate([t, d], axis=0), d, bd_mask, prec) for t, d in zip(ts, ds)]
        ts = [t + x[:C] for t, x in zip(ts, xs)]
        ds = [x[C:] for x in xs]
        span *= 2
    ts = [t + _gdot(t, d, bd_mask, prec) for t, d in zip(ts, ds)]
    size = INV_BASE
    while size < C:
        mask = (rowi // (2 * size) == colh // (2 * size)) & (rowi // size != colh // size)
        tos = [_gdot(t, jnp.where(mask, na, 0.0), bd_mask, prec) for t, na in zip(ts, neg_as)]
        ts = [t + _gdot(to, t, bd_mask, prec) for t, to in zip(ts, tos)]
        size *= 2
    return ts


def _rwkv_kernel(has_vmix, *refs):
    if has_vmix:
        (xs_ref, vfirst_ref, hs_ref, w0_ref, w2_ref, a0_ref, a2_ref, g2_ref, kk_ref,
         ka_ref, rk_ref, lng_ref, lnb_ref, v0_ref, v1_ref, v2_ref,
         out_ref, state_ref) = refs
    else:
        (xs_ref, hs_ref, w0_ref, w2_ref, a0_ref, a2_ref, g2_ref, kk_ref,
         ka_ref, rk_ref, lng_ref, lnb_ref, out_ref, vout_ref, state_ref) = refs
    C, W, N = CHUNK, RWKV_WIDTH, HEAD_DIM
    R = xs_ref.shape[0]
    RC = R * C

    @pl.when(pl.program_id(1) == 0)
    def _():
        state_ref[...] = jnp.zeros_like(state_ref)

    xs = xs_ref[...].reshape(RC, xs_ref.shape[2])
    r = xs[:, 0:W]
    k = xs[:, W:2 * W]
    v = xs[:, 2 * W:3 * W]
    wd = xs[:, 3 * W:3 * W + DECAY_LORA]
    ad = xs[:, 3 * W + DECAY_LORA:3 * W + DECAY_LORA + AAA_LORA]
    gd = xs[:, 3 * W + DECAY_LORA + AAA_LORA:]

    logd = -DECAY_SCALE * jax.nn.sigmoid(w0_ref[...] + _dot(jnp.tanh(wd), w2_ref[...]))
    a = jax.nn.sigmoid(a0_ref[...] + _dot(ad, a2_ref[...]))
    g = _dot(jax.nn.sigmoid(gd), g2_ref[...])

    ones_bd = hs_ref[...]
    GW = ones_bd.shape[0]
    bd_mask = (lax.broadcasted_iota(jnp.int32, (GW, GW), 0) // N
               == lax.broadcasted_iota(jnp.int32, (GW, GW), 1) // N)

    def head_sum(x, prec):
        xs_ = jnp.concatenate([x[:, i * GW:(i + 1) * GW] for i in range(W // GW)], axis=0)
        s = _pdot(_split(xs_, prec), [ones_bd])
        return jnp.concatenate([s[i * RC:(i + 1) * RC] for i in range(W // GW)], axis=1)

    kk = k * kk_ref[...]
    kk = kk * lax.rsqrt(jnp.maximum(head_sum(kk * kk, PREC_KNORM), 1e-24))
    k = k * (1.0 + (a - 1.0) * ka_ref[...])
    if has_vmix:
        gate = jax.nn.sigmoid(v0_ref[...] + _dot(_dot(v, v1_ref[...]), v2_ref[...]))
        v = v + (vfirst_ref[...].reshape(RC, W) - v) * gate
    else:
        vout_ref[...] = v.reshape(R, C, W)

    brow = lax.broadcasted_iota(jnp.int32, (RC, RC), 0)
    bcol = lax.broadcasted_iota(jnp.int32, (RC, RC), 1)
    tri = ((brow >= bcol) & (brow // C == bcol // C)).astype(CHUNK_DT)
    cum = _pdot([tri], _split(logd, PREC_CUM))
    decay_end = [jnp.exp(cum[(b + 1) * C - 1:(b + 1) * C, :]) for b in range(R)]
    grow = jnp.exp(-cum)
    kka = kk * a
    r_t = r * jnp.exp(cum)
    a_t = -kk * jnp.exp(cum - logd)
    k_t = k * grow
    b_t = kka * grow

    slabs = [(b, i) for b in range(R) for i in range(W // GW)]

    def cut(x, b, i):
        return x[b * C:(b + 1) * C, i * GW:(i + 1) * GW]

    rowi = lax.broadcasted_iota(jnp.int32, (C, GW), 0)
    colh = lax.broadcasted_iota(jnp.int32, (C, GW), 1) % N
    strict = rowi > colh
    incl = rowi >= colh
    ar = [jnp.concatenate([cut(a_t, b, i), cut(r_t, b, i)], axis=0) for b, i in slabs]
    am_b = [_gdot(x, cut(b_t, b, i), bd_mask, PREC_A, "nt") for x, (b, i) in zip(ar, slabs)]
    am_k = [_gdot(x, cut(k_t, b, i), bd_mask, PREC_A, "nt") for x, (b, i) in zip(ar, slabs)]
    tinvs = _unit_lower_inverse([jnp.where(strict, m[:C], 0.0) for m in am_b], rowi, colh, bd_mask, PREC_INV)
    a_ak_rk = [jnp.concatenate([jnp.where(strict, m[:C], 0.0), jnp.where(incl, m[C:], 0.0)], axis=0)
               for m in am_k]
    a_rb = [jnp.where(incl, m[C:], 0.0) for m in am_b]
    sts = [state_ref[b, i] for b, i in slabs]
    carry = [_gdot(x, st, bd_mask, PREC_APPLY, "nt") for x, st in zip(ar, sts)]
    intra = [_gdot(x, cut(v, b, i), bd_mask, PREC_APPLY) for x, (b, i) in zip(a_ak_rk, slabs)]
    us = [_gdot(ti, cr[:C] + it[:C], bd_mask, PREC_APPLY)
          for ti, cr, it in zip(tinvs, carry, intra)]
    ys = [cr[C:] + it[C:] + _gdot(rb, u, bd_mask, PREC_APPLY)
          for cr, it, rb, u in zip(carry, intra, a_rb, us)]
    for (b, i), st, u in zip(slabs, sts, us):
        uv = jnp.concatenate([u, cut(v, b, i)], axis=0)
        dend = decay_end[b][:, i * GW:(i + 1) * GW]
        bk = jnp.concatenate([cut(b_t, b, i), cut(k_t, b, i)], axis=0) * dend
        full = _pdot(_split(uv, PREC_APPLY), _split(bk, PREC_APPLY), "tn")
        full = jnp.where(bd_mask, full, 0.0)
        st = dend * st
        for h in range(GROUP):
            st = st + full[h * N:(h + 1) * N, :]
        state_ref[b, i] = st
    y = jnp.concatenate([jnp.concatenate(ys[b * (W // GW):(b + 1) * (W // GW)], axis=1)
                         for b in range(R)], axis=0)

    mean = head_sum(y, PREC_SUM) * (1.0 / N)
    yc = y - mean
    var = head_sum(yc * yc, PREC_SUM) * (1.0 / N)
    y = yc * lax.rsqrt(var + LNX_EPS) * lng_ref[...] + lnb_ref[...]
    bonus = head_sum(r * k * rk_ref[...], PREC_SUM) * v
    out_ref[...] = ((y + bonus) * g).reshape(R, C, W)


def _rwkv(rest, vfirst, p, batch, seq):
    n, shift_width = rest.shape
    nc = seq // CHUNK
    W = RWKV_WIDTH
    rows = RWKV_ROWS if batch % RWKV_ROWS == 0 else 1
    tok = lambda b, c: (b, c, 0)
    const = lambda b, c: (0, 0)
    vec = lambda a: a.reshape(1, -1)
    has_vmix = vfirst is not None
    full = lambda a: pl.BlockSpec(a.shape, const)
    gw = GROUP * HEAD_DIM
    hidx = jnp.arange(gw) // HEAD_DIM
    bd_mask = (hidx[:, None] == hidx[None, :]).astype(BF16)

    args = [rest.reshape(batch, seq, shift_width)]
    specs = [pl.BlockSpec((rows, CHUNK, shift_width), tok)]
    if has_vmix:
        args.append(vfirst.reshape(batch, seq, W))
        specs.append(pl.BlockSpec((rows, CHUNK, W), tok))
    params = [bd_mask, vec(p["w0"]), p["w2"], vec(p["a0"]), p["a2"], p["g2"],
              vec(p["k_k"]), vec(p["k_a"]), vec(p["r_k"]), vec(p["lnx_g"]), vec(p["lnx_b"])]
    if has_vmix:
        params += [vec(p["v0"]), p["v1"], p["v2"]]
    args += params
    specs += [full(a) for a in params]

    out_spec = pl.BlockSpec((rows, CHUNK, W), tok)
    out_sds = jax.ShapeDtypeStruct((batch, seq, W), F32)
    outs = pl.pallas_call(
        functools.partial(_rwkv_kernel, has_vmix),
        grid=(batch // rows, nc),
        in_specs=specs,
        out_specs=out_spec if has_vmix else [out_spec, out_spec],
        out_shape=out_sds if has_vmix else [out_sds, out_sds],
        scratch_shapes=[pltpu.VMEM((rows, W // gw, HEAD_DIM, gw), F32)],
        compiler_params=pltpu.CompilerParams(
            dimension_semantics=("arbitrary", "arbitrary"), vmem_limit_bytes=VMEM_LIMIT),
        name="rwkv_vmix" if has_vmix else "rwkv",
    )(*args)
    if has_vmix:
        return outs.reshape(n, W), vfirst
    return outs[0].reshape(n, W), outs[1].reshape(n, W)


def _ffn_kernel(final, *refs):
    if final:
        (x_ref, att_ref, rw_ref, wo_ref, n2g_ref, wg_ref, wu_ref, cw_ref, cb_ref, wd_ref, fg_ref,
         out_ref, halo_ref) = refs
    else:
        (x_ref, att_ref, rw_ref, wo_ref, n2g_ref, wg_ref, wu_ref, cw_ref, cb_ref, wd_ref,
         out_ref, halo_ref) = refs
    tm = x_ref.shape[0]

    @pl.when(pl.program_id(1) == 0)
    def _():
        halo_ref[...] = jnp.zeros_like(halo_ref)

    x = (x_ref[...] + _dot(att_ref[...], wo_ref[:ATT_WIDTH, :])
         + _dot(rw_ref[...], wo_ref[ATT_WIDTH:, :]))
    h = _rms_norm(x, n2g_ref[...]).astype(BF16)
    u = jnp.dot(h, wg_ref[...], preferred_element_type=F32)
    up = jnp.dot(h, wu_ref[...], preferred_element_type=F32)
    ext = jnp.concatenate([halo_ref[...], u], axis=0)
    halo_ref[...] = u[tm - SUBLANES:, :]
    c = (cb_ref[...] + cw_ref[0:1, :] * ext[SUBLANES - 2:SUBLANES - 2 + tm, :]
         + cw_ref[1:2, :] * ext[SUBLANES - 1:SUBLANES - 1 + tm, :] + cw_ref[2:3, :] * u)
    act = (c * jax.nn.sigmoid(c)) * up
    x = x + _dot(act, wd_ref[...])
    if final:
        x = _rms_norm(x, fg_ref[...])
    out_ref[...] = x


def _ffn(xf, att, rw, wo, n2g, wg, wu, cw, cb, wd, fg, batch, seq, tm):
    n, d = xf.shape
    ff = wg.shape[1]
    nt = seq // tm
    tok = lambda b, t: (b * nt + t, 0)
    const = lambda b, t: (0, 0)
    resident = lambda shape: pl.BlockSpec(shape, const, pipeline_mode=pl.Buffered(1))
    final = fg is not None
    args = [xf, att, rw, wo, n2g.reshape(1, d), wg, wu, cw, cb.reshape(1, ff), wd]
    specs = [pl.BlockSpec((tm, d), tok),
             pl.BlockSpec((tm, ATT_WIDTH), tok),
             pl.BlockSpec((tm, RWKV_WIDTH), tok),
             resident(wo.shape), resident((1, d)), resident(wg.shape), resident(wu.shape),
             resident(cw.shape), resident((1, ff)), resident(wd.shape)]
    if final:
        args.append(fg.reshape(1, d))
        specs.append(resident((1, d)))
    return pl.pallas_call(
        functools.partial(_ffn_kernel, final),
        grid=(batch, nt),
        in_specs=specs,
        out_specs=pl.BlockSpec((tm, d), tok),
        out_shape=jax.ShapeDtypeStruct((n, d), F32),
        scratch_shapes=[pltpu.VMEM((SUBLANES, ff), F32)],
        compiler_params=pltpu.CompilerParams(
            dimension_semantics=("arbitrary", "arbitrary"), vmem_limit_bytes=VMEM_LIMIT),
        name="ffn_final" if final else "ffn",
    )(*args)


def kernel(x, norm1_g, w_in, attn_sinks, shift_mu, w0, w2, a0, a2, g2, k_k, k_a, r_k, lnx_g, lnx_b,
           v0, v1, v2, w_out, norm2_g, ffn_w_gate, ffn_w_up, conv_w, conv_b, ffn_w_down, final_g):
    batch, seq, d = x.shape
    depth = w_in.shape[0]
    assert seq % WINDOW == 0 and seq % CHUNK == 0
    tm_proj = min(512, seq)
    tm_ffn = min(256, seq)
    xf = x.reshape(batch * seq, d)
    vfirst = None
    for l in range(depth):
        q, kv, rest = _in_proj(xf, norm1_g[l], w_in[l].astype(BF16), shift_mu[l], seq, tm_proj)
        att = _attention(q, kv, attn_sinks[l], batch, seq)
        p = dict(w0=w0[l], w2=w2[l], a0=a0[l], a2=a2[l], g2=g2[l],
                 k_k=k_k[l], k_a=k_a[l], r_k=r_k[l], lnx_g=lnx_g[l], lnx_b=lnx_b[l])
        if l > 0:
            p.update(v0=v0[l - 1], v1=v1[l - 1], v2=v2[l - 1])
        rw, vfirst = _rwkv(rest, vfirst, p, batch, seq)
        xf = _ffn(xf, att, rw, w_out[l].astype(BF16), norm2_g[l], ffn_w_gate[l].astype(BF16),
                  ffn_w_up[l].astype(BF16), conv_w[l], conv_b[l], ffn_w_down[l].astype(BF16),
                  final_g if l == depth - 1 else None, batch, seq, tm_ffn)
    return xf.reshape(batch, seq, d)
```

```python
import functools
import math

import jax
import jax.numpy as jnp
from jax import lax
from jax.experimental import pallas as pl
from jax.experimental.pallas import tpu as pltpu

F32 = jnp.float32
BF16 = jnp.bfloat16

HEAD_DIM = 64
ATT_HEADS = 8
ATT_KV_HEADS = 2
ATT_GROUP = ATT_HEADS // ATT_KV_HEADS
ATT_WIDTH = ATT_HEADS * HEAD_DIM
KV_WIDTH = ATT_KV_HEADS * HEAD_DIM
WINDOW = 128
RWKV_HEADS = 8
RWKV_WIDTH = RWKV_HEADS * HEAD_DIM
DECAY_LORA = 64
AAA_LORA = 64
GATE_LORA = 128
ATT_IN = ATT_WIDTH + 2 * KV_WIDTH
CONV_WIDTH = 3
RMS_EPS = 1e-5
LNX_EPS = 64e-5
DECAY_SCALE = math.exp(-0.5)
NEG_INF = -1e30

CHUNK = 64
INV_BASE = 8
SUBLANES = 8
RWKV_ROWS = 4
CHUNK_DT = BF16
PREC_A = 1
PREC_INV = 1
PREC_APPLY = 1
PREC_KNORM = 1
PREC_SUM = 1
PREC_CUM = 2
GROUP = 4
VMEM_LIMIT = 56 * 1024 * 1024


def _dot(a, b):
    return jnp.dot(a.astype(BF16), b.astype(BF16), preferred_element_type=F32)


def _rms_norm(x, g):
    ms = jnp.mean(x * x, axis=-1, keepdims=True)
    return x * lax.rsqrt(ms + RMS_EPS) * g


def _in_proj_kernel(tiles_per_seq, x_ref, g_ref, w_ref, mu_ref, q_ref, kv_ref, shifted_ref, halo_ref):
    tm = x_ref.shape[0]

    @pl.when(pl.program_id(0) % tiles_per_seq == 0)
    def _():
        halo_ref[...] = jnp.zeros_like(halo_ref)

    h = _rms_norm(x_ref[...], g_ref[...]).astype(BF16)
    q_ref[...] = jnp.dot(h, w_ref[:, :ATT_WIDTH], preferred_element_type=F32)
    kv_ref[...] = jnp.dot(h, w_ref[:, ATT_WIDTH:ATT_IN], preferred_element_type=F32)
    rest = jnp.dot(h, w_ref[:, ATT_IN:], preferred_element_type=F32)
    trow = lax.broadcasted_iota(jnp.int32, rest.shape, 0)
    prev = jnp.where(trow == 0, halo_ref[SUBLANES - 1:SUBLANES, :], pltpu.roll(rest, 1, 0))
    halo_ref[...] = rest[tm - SUBLANES:, :]
    shifted_ref[...] = rest + (prev - rest) * mu_ref[...]


def _in_proj(xf, g, w, mu, seq, tm):
    n, d = xf.shape
    in_width = w.shape[1]
    shift_width = in_width - ATT_IN
    const = lambda i: (0, 0)
    row = lambda i: (i, 0)
    return pl.pallas_call(
        functools.partial(_in_proj_kernel, seq // tm),
        grid=(n // tm,),
        in_specs=[pl.BlockSpec((tm, d), row),
                  pl.BlockSpec((1, d), const),
                  pl.BlockSpec((d, in_width), const),
                  pl.BlockSpec((1, shift_width), const)],
        out_specs=[pl.BlockSpec((tm, ATT_WIDTH), row),
                   pl.BlockSpec((tm, 2 * KV_WIDTH), row),
                   pl.BlockSpec((tm, shift_width), row)],
        out_shape=[jax.ShapeDtypeStruct((n, ATT_WIDTH), F32),
                   jax.ShapeDtypeStruct((n, 2 * KV_WIDTH), F32),
                   jax.ShapeDtypeStruct((n, shift_width), F32)],
        scratch_shapes=[pltpu.VMEM((SUBLANES, shift_width), F32)],
        compiler_params=pltpu.CompilerParams(
            dimension_semantics=("arbitrary",), vmem_limit_bytes=VMEM_LIMIT),
        name="in_proj",
    )(xf, g.reshape(1, d), w, mu.reshape(1, shift_width))


def _attn_kernel(sink_ref, q_ref, kvc_ref, kvp_ref, o_ref):
    blk = pl.program_id(1)
    Wn, dh = WINDOW, HEAD_DIM
    q = (q_ref[...] * (dh ** -0.5)).astype(BF16)
    kv = jnp.concatenate([kvp_ref[...], kvc_ref[...]], axis=0)
    qpos = lax.broadcasted_iota(jnp.int32, (Wn, 2 * Wn), 0)
    kpos = lax.broadcasted_iota(jnp.int32, (Wn, 2 * Wn), 1)
    dist = qpos + Wn - kpos
    valid = (dist >= 0) & (dist < Wn) & ((kpos >= Wn) | (blk > 0))
    distf = dist.astype(F32)
    lane = lax.broadcasted_iota(jnp.int32, (Wn, 2 * dh), 1)
    low = lane < dh
    ones = jnp.ones((2 * Wn, dh), F32)

    scores = []
    for j in range(ATT_KV_HEADS):
        qs = jnp.concatenate([q[:, h * dh:(h + 1) * dh]
                              for h in range(j * ATT_GROUP, (j + 1) * ATT_GROUP)], axis=0)
        k = kv[:, j * dh:(j + 1) * dh].astype(BF16)
        scores.append(lax.dot_general(qs, k, (((1,), (1,)), ((), ())), preferred_element_type=F32))

    probs, sink_terms = [], []
    for j in range(ATT_KV_HEADS):
        pj, cj = [], []
        for g in range(ATT_GROUP):
            h = j * ATT_GROUP + g
            s = scores[j][g * Wn:(g + 1) * Wn, :] - (2.0 ** (-8.0 * (h + 1) / ATT_HEADS)) * distf
            s = jnp.where(valid, s, NEG_INF)
            sink = sink_ref[h]
            m = jnp.maximum(jnp.max(s, axis=-1, keepdims=True), sink)
            pj.append(jnp.exp(s - m).astype(BF16))
            cj.append(jnp.exp(sink - m))
        probs.append(pj)
        sink_terms.append(cj)

    cols = []
    for j in range(ATT_KV_HEADS):
        v = kv[:, KV_WIDTH + j * dh:KV_WIDTH + (j + 1) * dh]
        v_even = jnp.concatenate([v, ones], axis=1).astype(BF16)
        v_odd = jnp.concatenate([ones, v], axis=1).astype(BF16)
        pe = jnp.concatenate(probs[j][0::2], axis=0)
        po = jnp.concatenate(probs[j][1::2], axis=0)
        oe = jnp.dot(pe, v_even, preferred_element_type=F32)
        oo = jnp.dot(po, v_odd, preferred_element_type=F32)
        for pair in range(ATT_GROUP // 2):
            e = oe[pair * Wn:(pair + 1) * Wn, :]
            o = oo[pair * Wn:(pair + 1) * Wn, :]
            num = jnp.where(low, e, o)
            den = pltpu.roll(jnp.where(low, o, e), dh, 1)
            den = den + jnp.where(low, sink_terms[j][2 * pair], sink_terms[j][2 * pair + 1])
            cols.append(num / den)
    o_ref[...] = jnp.concatenate(cols, axis=1)


def _attention(q, kv, sinks, batch, seq):
    nb = seq // WINDOW
    cur = lambda b, i: (b * nb + i, 0)
    prev = lambda b, i: (b * nb + jnp.maximum(i - 1, 0), 0)
    return pl.pallas_call(
        _attn_kernel,
        grid=(batch, nb),
        in_specs=[pl.BlockSpec(memory_space=pltpu.SMEM),
                  pl.BlockSpec((WINDOW, ATT_WIDTH), cur),
                  pl.BlockSpec((WINDOW, 2 * KV_WIDTH), cur),
                  pl.BlockSpec((WINDOW, 2 * KV_WIDTH), prev)],
        out_specs=pl.BlockSpec((WINDOW, ATT_WIDTH), cur),
        out_shape=jax.ShapeDtypeStruct((batch * seq, ATT_WIDTH), F32),
        compiler_params=pltpu.CompilerParams(
            dimension_semantics=("arbitrary", "arbitrary"), vmem_limit_bytes=VMEM_LIMIT),
        name="attn",
    )(sinks, q, kv, kv)


def _split(x, n):
    parts, rem = [], x
    for i in range(n):
        p = rem.astype(CHUNK_DT)
        parts.append(p)
        if i + 1 < n:
            rem = rem - p.astype(F32)
    return parts


_DIMS = {"nn": (((1,), (0,)), ((), ())), "nt": (((1,), (1,)), ((), ())), "tn": (((0,), (0,)), ((), ()))}


def _pdot(ap, bp, dims="nn"):
    order = max(len(ap), len(bp))
    acc = None
    for i, a in enumerate(ap):
        for j, b in enumerate(bp):
            if i + j < order:
                t = lax.dot_general(a, b, _DIMS[dims], preferred_element_type=F32)
                acc = t if acc is None else acc + t
    return acc


def _block_diag(parts, bd_mask):
    return [jnp.where(bd_mask, jnp.concatenate([p] * GROUP, axis=0), jnp.zeros((), p.dtype)) for p in parts]


def _gdot(x, y, bd_mask, prec, dims="nn"):
    ybd = jnp.where(bd_mask, jnp.concatenate([y] * GROUP, axis=0), 0.0)
    return _pdot(_split(x, prec), _split(ybd, prec), dims)


def _unit_lower_inverse(neg_as, rowi, colh, bd_mask, prec):
    C = CHUNK
    eye = (rowi == colh).astype(F32)
    ds = [jnp.where(rowi // INV_BASE == colh // INV_BASE, na, 0.0) for na in neg_as]
    ts = [eye + d for d in ds]
    ds = [_gdot(d, d, bd_mask, prec) for d in ds]
    span = 4
    while span < INV_BASE:
        xs = [_gdot(jnp.concatenate([t, d], axis=0), d, bd_mask, prec) for t, d in zip(ts, ds)]
        ts = [t + x[:C] for t, x in zip(ts, xs)]
        ds = [x[C:] for x in xs]
        span *= 2
    ts = [t + _gdot(t, d, bd_mask, prec) for t, d in zip(ts, ds)]
    size = INV_BASE
    while size < C:
        mask = (rowi // (2 * size) == colh // (2 * size)) & (rowi // size != colh // size)
        tos = [_gdot(t, jnp.where(mask, na, 0.0), bd_mask, prec) for t, na in zip(ts, neg_as)]
        ts = [t + _gdot(to, t, bd_mask, prec) for t, to in zip(ts, tos)]
        size *= 2
    return ts


def _rwkv_kernel(has_vmix, *refs):
    if has_vmix:
        (xs_ref, vfirst_ref, hs_ref, w0_ref, w2_ref, a0_ref, a2_ref, g2_ref, kk_ref,
         ka_ref, rk_ref, lng_ref, lnb_ref, v0_ref, v1_ref, v2_ref,
         out_ref, state_ref) = refs
    else:
        (xs_ref, hs_ref, w0_ref, w2_ref, a0_ref, a2_ref, g2_ref, kk_ref,
         ka_ref, rk_ref, lng_ref, lnb_ref, out_ref, vout_ref, state_ref) = refs
    C, W, N = CHUNK, RWKV_WIDTH, HEAD_DIM
    R = xs_ref.shape[0]
    RC = R * C

    @pl.when(pl.program_id(1) == 0)
    def _():
        state_ref[...] = jnp.zeros_like(state_ref)

    xs = xs_ref[...].reshape(RC, xs_ref.shape[2])
    r = xs[:, 0:W]
    k = xs[:, W:2 * W]
    v = xs[:, 2 * W:3 * W]
    wd = xs[:, 3 * W:3 * W + DECAY_LORA]
    ad = xs[:, 3 * W + DECAY_LORA:3 * W + DECAY_LORA + AAA_LORA]
    gd = xs[:, 3 * W + DECAY_LORA + AAA_LORA:]

    logd = -DECAY_SCALE * jax.nn.sigmoid(w0_ref[...] + _dot(jnp.tanh(wd), w2_ref[...]))
    a = jax.nn.sigmoid(a0_ref[...] + _dot(ad, a2_ref[...]))
    g = _dot(jax.nn.sigmoid(gd), g2_ref[...])

    ones_bd = hs_ref[...]
    GW = ones_bd.shape[0]
    bd_mask = (lax.broadcasted_iota(jnp.int32, (GW, GW), 0) // N
               == lax.broadcasted_iota(jnp.int32, (GW, GW), 1) // N)

    def head_sum(x, prec):
        xs_ = jnp.concatenate([x[:, i * GW:(i + 1) * GW] for i in range(W // GW)], axis=0)
        s = _pdot(_split(xs_, prec), [ones_bd])
        return jnp.concatenate([s[i * RC:(i + 1) * RC] for i in range(W // GW)], axis=1)

    kk = k * kk_ref[...]
    kk = kk * lax.rsqrt(jnp.maximum(head_sum(kk * kk, PREC_KNORM), 1e-24))
    k = k * (1.0 + (a - 1.0) * ka_ref[...])
    if has_vmix:
        gate = jax.nn.sigmoid(v0_ref[...] + _dot(_dot(v, v1_ref[...]), v2_ref[...]))
        v = v + (vfirst_ref[...].reshape(RC, W) - v) * gate
    else:
        vout_ref[...] = v.reshape(R, C, W)

    brow = lax.broadcasted_iota(jnp.int32, (RC, RC), 0)
    bcol = lax.broadcasted_iota(jnp.int32, (RC, RC), 1)
    tri = ((brow >= bcol) & (brow // C == bcol // C)).astype(CHUNK_DT)
    cum = _pdot([tri], _split(logd, PREC_CUM))
    decay_end = [jnp.exp(cum[(b + 1) * C - 1:(b + 1) * C, :]) for b in range(R)]
    grow = jnp.exp(-cum)
    kka = kk * a
    r_t = r * jnp.exp(cum)
    a_t = -kk * jnp.exp(cum - logd)
    k_t = k * grow
    b_t = kka * grow

    slabs = [(b, i) for b in range(R) for i in range(W // GW)]

    def cut(x, b, i):
        return x[b * C:(b + 1) * C, i * GW:(i + 1) * GW]

    rowi = lax.broadcasted_iota(jnp.int32, (C, GW), 0)
    colh = lax.broadcasted_iota(jnp.int32, (C, GW), 1) % N
    strict = rowi > colh
    incl = rowi >= colh
    ar = [jnp.concatenate([cut(a_t, b, i), cut(r_t, b, i)], axis=0) for b, i in slabs]
    am_b = [_gdot(x, cut(b_t, b, i), bd_mask, PREC_A, "nt") for x, (b, i) in zip(ar, slabs)]
    am_k = [_gdot(x, cut(k_t, b, i), bd_mask, PREC_A, "nt") for x, (b, i) in zip(ar, slabs)]
    tinvs = _unit_lower_inverse([jnp.where(strict, m[:C], 0.0) for m in am_b], rowi, colh, bd_mask, PREC_INV)
    a_ak_rk = [jnp.concatenate([jnp.where(strict, m[:C], 0.0), jnp.where(incl, m[C:], 0.0)], axis=0)
               for m in am_k]
    a_rb = [jnp.where(incl, m[C:], 0.0) for m in am_b]
    sts = [state_ref[b, i] for b, i in slabs]
    carry = [_gdot(x, st, bd_mask, PREC_APPLY, "nt") for x, st in zip(ar, sts)]
    intra = [_gdot(x, cut(v, b, i), bd_mask, PREC_APPLY) for x, (b, i) in zip(a_ak_rk, slabs)]
    us = [_gdot(ti, cr[:C] + it[:C], bd_mask, PREC_APPLY)
          for ti, cr, it in zip(tinvs, carry, intra)]
    ys = [cr[C:] + it[C:] + _gdot(rb, u, bd_mask, PREC_APPLY)
          for cr, it, rb, u in zip(carry, intra, a_rb, us)]
    for (b, i), st, u in zip(slabs, sts, us):
        uv = jnp.concatenate([u, cut(v, b, i)], axis=0)
        dend = decay_end[b][:, i * GW:(i + 1) * GW]
        bk = jnp.concatenate([cut(b_t, b, i), cut(k_t, b, i)], axis=0) * dend
        full = _pdot(_split(uv, PREC_APPLY), _split(bk, PREC_APPLY), "tn")
        full = jnp.where(bd_mask, full, 0.0)
        st = dend * st
        for h in range(GROUP):
            st = st + full[h * N:(h + 1) * N, :]
        state_ref[b, i] = st
    y = jnp.concatenate([jnp.concatenate(ys[b * (W // GW):(b + 1) * (W // GW)], axis=1)
                         for b in range(R)], axis=0)

    mean = head_sum(y, PREC_SUM) * (1.0 / N)
    yc = y - mean
    var = head_sum(yc * yc, PREC_SUM) * (1.0 / N)
    y = yc * lax.rsqrt(var + LNX_EPS) * lng_ref[...] + lnb_ref[...]
    bonus = head_sum(r * k * rk_ref[...], PREC_SUM) * v
    out_ref[...] = ((y + bonus) * g).reshape(R, C, W)


def _rwkv(rest, vfirst, p, batch, seq):
    n, shift_width = rest.shape
    nc = seq // CHUNK
    W = RWKV_WIDTH
    rows = RWKV_ROWS if batch % RWKV_ROWS == 0 else 1
    tok = lambda b, c: (b, c, 0)
    const = lambda b, c: (0, 0)
    vec = lambda a: a.reshape(1, -1)
    has_vmix = vfirst is not None
    full = lambda a: pl.BlockSpec(a.shape, const)
    gw = GROUP * HEAD_DIM
    hidx = jnp.arange(gw) // HEAD_DIM
    bd_mask = (hidx[:, None] == hidx[None, :]).astype(BF16)

    args = [rest.reshape(batch, seq, shift_width)]
    specs = [pl.BlockSpec((rows, CHUNK, shift_width), tok)]
    if has_vmix:
        args.append(vfirst.reshape(batch, seq, W))
        specs.append(pl.BlockSpec((rows, CHUNK, W), tok))
    params = [bd_mask, vec(p["w0"]), p["w2"], vec(p["a0"]), p["a2"], p["g2"],
              vec(p["k_k"]), vec(p["k_a"]), vec(p["r_k"]), vec(p["lnx_g"]), vec(p["lnx_b"])]
    if has_vmix:
        params += [vec(p["v0"]), p["v1"], p["v2"]]
    args += params
    specs += [full(a) for a in params]

    out_spec = pl.BlockSpec((rows, CHUNK, W), tok)
    out_sds = jax.ShapeDtypeStruct((batch, seq, W), F32)
    outs = pl.pallas_call(
        functools.partial(_rwkv_kernel, has_vmix),
        grid=(batch // rows, nc),
        in_specs=specs,
        out_specs=out_spec if has_vmix else [out_spec, out_spec],
        out_shape=out_sds if has_vmix else [out_sds, out_sds],
        scratch_shapes=[pltpu.VMEM((rows, W // gw, HEAD_DIM, gw), F32)],
        compiler_params=pltpu.CompilerParams(
            dimension_semantics=("arbitrary", "arbitrary"), vmem_limit_bytes=VMEM_LIMIT),
        name="rwkv_vmix" if has_vmix else "rwkv",
    )(*args)
    if has_vmix:
        return outs.reshape(n, W), vfirst
    return outs[0].reshape(n, W), outs[1].reshape(n, W)


def _ffn_kernel(final, *refs):
    if final:
        (x_ref, att_ref, rw_ref, wo_ref, n2g_ref, wg_ref, wu_ref, cw_ref, cb_ref, wd_ref, fg_ref,
         out_ref, halo_ref) = refs
    else:
        (x_ref, att_ref, rw_ref, wo_ref, n2g_ref, wg_ref, wu_ref, cw_ref, cb_ref, wd_ref,
         out_ref, halo_ref) = refs
    tm = x_ref.shape[0]

    @pl.when(pl.program_id(1) == 0)
    def _():
        halo_ref[...] = jnp.zeros_like(halo_ref)

    x = (x_ref[...] + _dot(att_ref[...], wo_ref[:ATT_WIDTH, :])
         + _dot(rw_ref[...], wo_ref[ATT_WIDTH:, :]))
    h = _rms_norm(x, n2g_ref[...]).astype(BF16)
    u = jnp.dot(h, wg_ref[...], preferred_element_type=F32)
    up = jnp.dot(h, wu_ref[...], preferred_element_type=F32)
    ext = jnp.concatenate([halo_ref[...], u], axis=0)
    halo_ref[...] = u[tm - SUBLANES:, :]
    c = (cb_ref[...] + cw_ref[0:1, :] * ext[SUBLANES - 2:SUBLANES - 2 + tm, :]
         + cw_ref[1:2, :] * ext[SUBLANES - 1:SUBLANES - 1 + tm, :] + cw_ref[2:3, :] * u)
    act = (c * jax.nn.sigmoid(c)) * up
    x = x + _dot(act, wd_ref[...])
    if final:
        x = _rms_norm(x, fg_ref[...])
    out_ref[...] = x


def _ffn(xf, att, rw, wo, n2g, wg, wu, cw, cb, wd, fg, batch, seq, tm):
    n, d = xf.shape
    ff = wg.shape[1]
    nt = seq // tm
    tok = lambda b, t: (b * nt + t, 0)
    const = lambda b, t: (0, 0)
    resident = lambda shape: pl.BlockSpec(shape, const, pipeline_mode=pl.Buffered(1))
    final = fg is not None
    args = [xf, att, rw, wo, n2g.reshape(1, d), wg, wu, cw, cb.reshape(1, ff), wd]
    specs = [pl.BlockSpec((tm, d), tok),
             pl.BlockSpec((tm, ATT_WIDTH), tok),
             pl.BlockSpec((tm, RWKV_WIDTH), tok),
             resident(wo.shape), resident((1, d)), resident(wg.shape), resident(wu.shape),
             resident(cw.shape), resident((1, ff)), resident(wd.shape)]
    if final:
        args.append(fg.reshape(1, d))
        specs.append(resident((1, d)))
    return pl.pallas_call(
        functools.partial(_ffn_kernel, final),
        grid=(batch, nt),
        in_specs=specs,
        out_specs=pl.BlockSpec((tm, d), tok),
        out_shape=jax.ShapeDtypeStruct((n, d), F32),
        scratch_shapes=[pltpu.VMEM((SUBLANES, ff), F32)],
        compiler_params=pltpu.CompilerParams(
            dimension_semantics=("arbitrary", "arbitrary"), vmem_limit_bytes=VMEM_LIMIT),
        name="ffn_final" if final else "ffn",
    )(*args)


def kernel(x, norm1_g, w_in, attn_sinks, shift_mu, w0, w2, a0, a2, g2, k_k, k_a, r_k, lnx_g, lnx_b,
           v0, v1, v2, w_out, norm2_g, ffn_w_gate, ffn_w_up, conv_w, conv_b, ffn_w_down, final_g):
    batch, seq, d = x.shape
    depth = w_in.shape[0]
    assert seq % WINDOW == 0 and seq % CHUNK == 0
    tm_proj = min(512, seq)
    tm_ffn = min(256, seq)
    xf = x.reshape(batch * seq, d)
    vfirst = None
    for l in range(depth):
        q, kv, rest = _in_proj(xf, norm1_g[l], w_in[l].astype(BF16), shift_mu[l], seq, tm_proj)
        att = _attention(q, kv, attn_sinks[l], batch, seq)
        p = dict(w0=w0[l], w2=w2[l], a0=a0[l], a2=a2[l], g2=g2[l],
                 k_k=k_k[l], k_a=k_a[l], r_k=r_k[l], lnx_g=lnx_g[l], lnx_b=lnx_b[l])
        if l > 0:
            p.update(v0=v0[l - 1], v1=v1[l - 1], v2=v2[l - 1])
        rw, vfirst = _rwkv(rest, vfirst, p, batch, seq)
        xf = _ffn(xf, att, rw, w_out[l].astype(BF16), norm2_g[l], ffn_w_gate[l].astype(BF16),
                  ffn_w_up[l].astype(BF16), conv_w[l], conv_b[l], ffn_w_down[l].astype(BF16),
                  final_g if l == depth - 1 else None, batch, seq, tm_ffn)
    return xf.reshape(batch, seq, d)
```

```python
import functools
import math

import jax
import jax.numpy as jnp
from jax import lax
from jax.experimental import pallas as pl
from jax.experimental.pallas import tpu as pltpu

F32 = jnp.float32
BF16 = jnp.bfloat16

HEAD_DIM = 64
ATT_HEADS = 8
ATT_KV_HEADS = 2
ATT_GROUP = ATT_HEADS // ATT_KV_HEADS
ATT_WIDTH = ATT_HEADS * HEAD_DIM
KV_WIDTH = ATT_KV_HEADS * HEAD_DIM
WINDOW = 128
RWKV_HEADS = 8
RWKV_WIDTH = RWKV_HEADS * HEAD_DIM
DECAY_LORA = 64
AAA_LORA = 64
GATE_LORA = 128
ATT_IN = ATT_WIDTH + 2 * KV_WIDTH
CONV_WIDTH = 3
RMS_EPS = 1e-5
LNX_EPS = 64e-5
DECAY_SCALE = math.exp(-0.5)
NEG_INF = -1e30

CHUNK = 64
INV_BASE = 8
SUBLANES = 8
RWKV_ROWS = 4
CHUNK_DT = BF16
PREC_A = 1
PREC_INV = 1
PREC_APPLY = 1
PREC_KNORM = 1
PREC_SUM = 1
PREC_CUM = 2
GROUP = 4
VMEM_LIMIT = 56 * 1024 * 1024


def _dot(a, b):
    return jnp.dot(a.astype(BF16), b.astype(BF16), preferred_element_type=F32)


def _rms_norm(x, g):
    ms = jnp.mean(x * x, axis=-1, keepdims=True)
    return x * lax.rsqrt(ms + RMS_EPS) * g


def _proj_attn_kernel(tiles_per_seq, n_tiles, sink_ref, x_ref, g_ref, w_ref, mu_ref,
                      att_ref, shifted_ref, halo_ref, halo_used_ref, q_scr, kv_scr, q_stage, kv_stage):
    i = pl.program_id(0)
    tm = x_ref.shape[0]
    Wn, dh = WINDOW, HEAD_DIM
    nblk = tm // Wn

    @pl.when(i == 0)
    def _():
        q_stage[...] = jnp.zeros_like(q_stage)
        kv_stage[...] = jnp.zeros_like(kv_stage)
        halo_used_ref[...] = jnp.zeros_like(halo_used_ref)

    @pl.when(i % tiles_per_seq == 0)
    def _():
        halo_ref[...] = jnp.zeros_like(halo_ref)

    seq_start = (i + tiles_per_seq - 1) % tiles_per_seq == 0
    q_scr[...] = q_stage[...]
    kv_scr[...] = kv_stage[...]
    q_prev = q_scr[...]
    kv_prev = kv_scr[...]
    qpos = lax.broadcasted_iota(jnp.int32, (Wn, 2 * Wn), 0)
    kpos = lax.broadcasted_iota(jnp.int32, (Wn, 2 * Wn), 1)
    dist = qpos + Wn - kpos
    in_window = (dist >= 0) & (dist < Wn)
    distf = dist.astype(F32)
    lane = lax.broadcasted_iota(jnp.int32, (Wn, 2 * dh), 1)
    low = lane < dh
    scores = {}
    for b in range(nblk):
        for j in range(ATT_KV_HEADS):
            qs = jnp.concatenate([q_prev[b * Wn:(b + 1) * Wn, h * dh:(h + 1) * dh]
                                  for h in range(j * ATT_GROUP, (j + 1) * ATT_GROUP)], axis=0)
            k = kv_prev[b * Wn:(b + 2) * Wn, j * dh:(j + 1) * dh]
            scores[b, j] = lax.dot_general(qs, k, (((1,), (1,)), ((), ())), preferred_element_type=F32)

    hn = _rms_norm(x_ref[...], g_ref[...]).astype(BF16)
    q_new = jnp.dot(hn, w_ref[:, :ATT_WIDTH], preferred_element_type=F32)
    kv_new = jnp.dot(hn, w_ref[:, ATT_WIDTH:ATT_IN], preferred_element_type=F32)
    rest = jnp.dot(hn, w_ref[:, ATT_IN:], preferred_element_type=F32)
    halo = jnp.where(i == n_tiles, halo_used_ref[...], halo_ref[...])
    halo_used_ref[...] = halo
    trow = lax.broadcasted_iota(jnp.int32, rest.shape, 0)
    prev = jnp.where(trow == 0, halo[SUBLANES - 1:SUBLANES, :], pltpu.roll(rest, 1, 0))
    halo_ref[...] = rest[tm - SUBLANES:, :]
    shifted_ref[...] = rest + (prev - rest) * mu_ref[...]

    probs, sink_terms = {}, {}
    for b in range(nblk):
        valid = in_window & ((kpos >= Wn) | jnp.logical_not(seq_start)) if b == 0 else in_window
        for j in range(ATT_KV_HEADS):
            for g in range(ATT_GROUP):
                h = j * ATT_GROUP + g
                s = scores[b, j][g * Wn:(g + 1) * Wn, :] - (2.0 ** (-8.0 * (h + 1) / ATT_HEADS)) * distf
                s = jnp.where(valid, s, NEG_INF)
                sink = sink_ref[h]
                m = jnp.maximum(jnp.max(s, axis=-1, keepdims=True), sink)
                probs[b, h] = jnp.exp(s - m).astype(BF16)
                sink_terms[b, h] = jnp.exp(sink - m)
    v_even = [kv_prev[:, KV_WIDTH + 4 * j * dh:KV_WIDTH + (4 * j + 2) * dh] for j in range(ATT_KV_HEADS)]
    v_odd = [kv_prev[:, KV_WIDTH + (4 * j + 2) * dh:KV_WIDTH + (4 * j + 4) * dh] for j in range(ATT_KV_HEADS)]
    rows = []
    for b in range(nblk):
        cols = []
        for j in range(ATT_KV_HEADS):
            hs = range(j * ATT_GROUP, (j + 1) * ATT_GROUP)
            pe = jnp.concatenate([probs[b, h] for h in hs[0::2]], axis=0)
            po = jnp.concatenate([probs[b, h] for h in hs[1::2]], axis=0)
            oe = jnp.dot(pe, v_even[j][b * Wn:(b + 2) * Wn], preferred_element_type=F32)
            oo = jnp.dot(po, v_odd[j][b * Wn:(b + 2) * Wn], preferred_element_type=F32)
            for pair in range(ATT_GROUP // 2):
                e = oe[pair * Wn:(pair + 1) * Wn, :]
                o = oo[pair * Wn:(pair + 1) * Wn, :]
                num = jnp.where(low, e, o)
                den = pltpu.roll(jnp.where(low, o, e), dh, 1)
                den = den + jnp.where(low, sink_terms[b, hs[2 * pair]], sink_terms[b, hs[2 * pair + 1]])
                cols.append(num / den)
        rows.append(jnp.concatenate(cols, axis=1))
    att_ref[...] = jnp.concatenate(rows, axis=0)

    ones = jnp.ones((tm, dh), F32)
    pieces = [kv_new[:, :KV_WIDTH]]
    for j in range(ATT_KV_HEADS):
        v = kv_new[:, KV_WIDTH + j * dh:KV_WIDTH + (j + 1) * dh]
        pieces += [jnp.concatenate([v, ones], axis=1), jnp.concatenate([ones, v], axis=1)]
    kv_stage[0:Wn, :] = kv_prev[tm:tm + Wn, :]
    kv_stage[Wn:, :] = jnp.concatenate(pieces, axis=1).astype(BF16)
    q_stage[...] = (q_new * (dh ** -0.5)).astype(BF16)


def _proj_attn(xf, g, w, mu, sinks, seq, tm):
    n, d = xf.shape
    in_width = w.shape[1]
    shift_width = in_width - ATT_IN
    n_tiles = n // tm
    const = lambda i: (0, 0)
    cur = lambda i: (jnp.minimum(i, n_tiles - 1), 0)
    lag = lambda i: (jnp.maximum(i - 1, 0), 0)
    return pl.pallas_call(
        functools.partial(_proj_attn_kernel, seq // tm, n_tiles),
        grid=(n_tiles + 1,),
        in_specs=[pl.BlockSpec(memory_space=pltpu.SMEM),
                  pl.BlockSpec((tm, d), cur),
                  pl.BlockSpec((1, d), const),
                  pl.BlockSpec((d, in_width), const),
                  pl.BlockSpec((1, shift_width), const)],
        out_specs=[pl.BlockSpec((tm, ATT_WIDTH), lag),
                   pl.BlockSpec((tm, shift_width), cur)],
        out_shape=[jax.ShapeDtypeStruct((n, ATT_WIDTH), F32),
                   jax.ShapeDtypeStruct((n, shift_width), F32)],
        scratch_shapes=[pltpu.VMEM((SUBLANES, shift_width), F32),
                        pltpu.VMEM((SUBLANES, shift_width), F32),
                        pltpu.VMEM((tm, ATT_WIDTH), BF16),
                        pltpu.VMEM((WINDOW + tm, 5 * KV_WIDTH), BF16),
                        pltpu.VMEM((tm, ATT_WIDTH), BF16),
                        pltpu.VMEM((WINDOW + tm, 5 * KV_WIDTH), BF16)],
        compiler_params=pltpu.CompilerParams(
            dimension_semantics=("arbitrary",), vmem_limit_bytes=VMEM_LIMIT),
        name="proj_attn",
    )(sinks, xf, g.reshape(1, d), w, mu.reshape(1, shift_width))


def _split(x, n):
    parts, rem = [], x
    for i in range(n):
        p = rem.astype(CHUNK_DT)
        parts.append(p)
        if i + 1 < n:
            rem = rem - p.astype(F32)
    return parts


_DIMS = {"nn": (((1,), (0,)), ((), ())), "nt": (((1,), (1,)), ((), ())), "tn": (((0,), (0,)), ((), ()))}


def _pdot(ap, bp, dims="nn"):
    order = max(len(ap), len(bp))
    acc = None
    for i, a in enumerate(ap):
        for j, b in enumerate(bp):
            if i + j < order:
                t = lax.dot_general(a, b, _DIMS[dims], preferred_element_type=F32)
                acc = t if acc is None else acc + t
    return acc


def _gdot(x, y, bd_mask, prec, dims="nn"):
    ybd = jnp.where(bd_mask, jnp.concatenate([y] * GROUP, axis=0), 0.0)
    return _pdot(_split(x, prec), _split(ybd, prec), dims)


def _unit_lower_inverse(neg_as, rowi, colh, bd_mask, prec):
    C = CHUNK
    eye = (rowi == colh).astype(F32)
    ds = [jnp.where(rowi // INV_BASE == colh // INV_BASE, na, 0.0) for na in neg_as]
    ts = [eye + d for d in ds]
    ds = [_gdot(d, d, bd_mask, prec) for d in ds]
    span = 4
    while span < INV_BASE:
        xs = [_gdot(jnp.concatenate([t, d], axis=0), d, bd_mask, prec) for t, d in zip(ts, ds)]
        ts = [t + x[:C] for t, x in zip(ts, xs)]
        ds = [x[C:] for x in xs]
        span *= 2
    ts = [t + _gdot(t, d, bd_mask, prec) for t, d in zip(ts, ds)]
    size = INV_BASE
    while size < C:
        mask = (rowi // (2 * size) == colh // (2 * size)) & (rowi // size != colh // size)
        tos = [_gdot(t, jnp.where(mask, na, 0.0), bd_mask, prec) for t, na in zip(ts, neg_as)]
        ts = [t + _gdot(to, t, bd_mask, prec) for t, to in zip(ts, tos)]
        size *= 2
    return ts


def _rwkv_kernel(has_vmix, *refs):
    if has_vmix:
        (xs_ref, vfirst_ref, hs_ref, w0_ref, w2_ref, a0_ref, a2_ref, g2_ref, kk_ref,
         ka_ref, rk_ref, lng_ref, lnb_ref, v0_ref, v1_ref, v2_ref,
         out_ref, state_ref) = refs
    else:
        (xs_ref, hs_ref, w0_ref, w2_ref, a0_ref, a2_ref, g2_ref, kk_ref,
         ka_ref, rk_ref, lng_ref, lnb_ref, out_ref, vout_ref, state_ref) = refs
    C, W, N = CHUNK, RWKV_WIDTH, HEAD_DIM
    R = xs_ref.shape[0]
    RC = R * C

    @pl.when(pl.program_id(1) == 0)
    def _():
        state_ref[...] = jnp.zeros_like(state_ref)

    xs = xs_ref[...].reshape(RC, xs_ref.shape[2])
    r = xs[:, 0:W]
    k = xs[:, W:2 * W]
    v = xs[:, 2 * W:3 * W]
    wd = xs[:, 3 * W:3 * W + DECAY_LORA]
    ad = xs[:, 3 * W + DECAY_LORA:3 * W + DECAY_LORA + AAA_LORA]
    gd = xs[:, 3 * W + DECAY_LORA + AAA_LORA:]

    logd = -DECAY_SCALE * jax.nn.sigmoid(w0_ref[...] + _dot(jnp.tanh(wd), w2_ref[...]))
    a = jax.nn.sigmoid(a0_ref[...] + _dot(ad, a2_ref[...]))
    g = _dot(jax.nn.sigmoid(gd), g2_ref[...])

    ones_bd = hs_ref[...]
    GW = ones_bd.shape[0]
    bd_mask = (lax.broadcasted_iota(jnp.int32, (GW, GW), 0) // N
               == lax.broadcasted_iota(jnp.int32, (GW, GW), 1) // N)

    def head_sum(x, prec):
        xs_ = jnp.concatenate([x[:, i * GW:(i + 1) * GW] for i in range(W // GW)], axis=0)
        s = _pdot(_split(xs_, prec), [ones_bd])
        return jnp.concatenate([s[i * RC:(i + 1) * RC] for i in range(W // GW)], axis=1)

    kk = k * kk_ref[...]
    kk = kk * lax.rsqrt(jnp.maximum(head_sum(kk * kk, PREC_KNORM), 1e-24))
    k = k * (1.0 + (a - 1.0) * ka_ref[...])
    if has_vmix:
        gate = jax.nn.sigmoid(v0_ref[...] + _dot(_dot(v, v1_ref[...]), v2_ref[...]))
        v = v + (vfirst_ref[...].reshape(RC, W) - v) * gate
    else:
        vout_ref[...] = v.reshape(R, C, W)

    brow = lax.broadcasted_iota(jnp.int32, (RC, RC), 0)
    bcol = lax.broadcasted_iota(jnp.int32, (RC, RC), 1)
    tri = ((brow >= bcol) & (brow // C == bcol // C)).astype(CHUNK_DT)
    cum = _pdot([tri], _split(logd, PREC_CUM))
    decay_end = [jnp.exp(cum[(b + 1) * C - 1:(b + 1) * C, :]) for b in range(R)]
    grow = jnp.exp(-cum)
    kka = kk * a
    r_t = r * jnp.exp(cum)
    a_t = -kk * jnp.exp(cum - logd)
    k_t = k * grow
    b_t = kka * grow

    slabs = [(b, i) for b in range(R) for i in range(W // GW)]

    def cut(x, b, i):
        return x[b * C:(b + 1) * C, i * GW:(i + 1) * GW]

    rowi = lax.broadcasted_iota(jnp.int32, (C, GW), 0)
    colh = lax.broadcasted_iota(jnp.int32, (C, GW), 1) % N
    strict = rowi > colh
    incl = rowi >= colh
    ar = [jnp.concatenate([cut(a_t, b, i), cut(r_t, b, i)], axis=0) for b, i in slabs]
    am_b = [_gdot(x, cut(b_t, b, i), bd_mask, PREC_A, "nt") for x, (b, i) in zip(ar, slabs)]
    am_k = [_gdot(x, cut(k_t, b, i), bd_mask, PREC_A, "nt") for x, (b, i) in zip(ar, slabs)]
    tinvs = _unit_lower_inverse([jnp.where(strict, m[:C], 0.0) for m in am_b], rowi, colh, bd_mask, PREC_INV)
    a_ak_rk = [jnp.concatenate([jnp.where(strict, m[:C], 0.0), jnp.where(incl, m[C:], 0.0)], axis=0)
               for m in am_k]
    a_rb = [jnp.where(incl, m[C:], 0.0) for m in am_b]
    sts = [state_ref[b, i] for b, i in slabs]
    carry = [_gdot(x, st, bd_mask, PREC_APPLY, "nt") for x, st in zip(ar, sts)]
    intra = [_gdot(x, cut(v, b, i), bd_mask, PREC_APPLY) for x, (b, i) in zip(a_ak_rk, slabs)]
    us = [_gdot(ti, cr[:C] + it[:C], bd_mask, PREC_APPLY)
          for ti, cr, it in zip(tinvs, carry, intra)]
    ys = [cr[C:] + it[C:] + _gdot(rb, u, bd_mask, PREC_APPLY)
          for cr, it, rb, u in zip(carry, intra, a_rb, us)]
    for (b, i), st, u in zip(slabs, sts, us):
        uv = jnp.concatenate([u, cut(v, b, i)], axis=0)
        dend = decay_end[b][:, i * GW:(i + 1) * GW]
        bk = jnp.concatenate([cut(b_t, b, i), cut(k_t, b, i)], axis=0) * dend
        full = _pdot(_split(uv, PREC_APPLY), _split(bk, PREC_APPLY), "tn")
        full = jnp.where(bd_mask, full, 0.0)
        st = dend * st
        for h in range(GROUP):
            st = st + full[h * N:(h + 1) * N, :]
        state_ref[b, i] = st
    y = jnp.concatenate([jnp.concatenate(ys[b * (W // GW):(b + 1) * (W // GW)], axis=1)
                         for b in range(R)], axis=0)

    mean = head_sum(y, PREC_SUM) * (1.0 / N)
    yc = y - mean
    var = head_sum(yc * yc, PREC_SUM) * (1.0 / N)
    y = yc * lax.rsqrt(var + LNX_EPS) * lng_ref[...] + lnb_ref[...]
    bonus = head_sum(r * k * rk_ref[...], PREC_SUM) * v
    out_ref[...] = ((y + bonus) * g).reshape(R, C, W)


def _rwkv(rest, vfirst, p, batch, seq):
    n, shift_width = rest.shape
    nc = seq // CHUNK
    W = RWKV_WIDTH
    rows = RWKV_ROWS if batch % RWKV_ROWS == 0 else 1
    tok = lambda b, c: (b, c, 0)
    const = lambda b, c: (0, 0)
    vec = lambda a: a.reshape(1, -1)
    has_vmix = vfirst is not None
    full = lambda a: pl.BlockSpec(a.shape, const)
    gw = GROUP * HEAD_DIM
    hidx = jnp.arange(gw) // HEAD_DIM
    bd_mask = (hidx[:, None] == hidx[None, :]).astype(BF16)

    args = [rest.reshape(batch, seq, shift_width)]
    specs = [pl.BlockSpec((rows, CHUNK, shift_width), tok)]
    if has_vmix:
        args.append(vfirst.reshape(batch, seq, W))
        specs.append(pl.BlockSpec((rows, CHUNK, W), tok))
    params = [bd_mask, vec(p["w0"]), p["w2"], vec(p["a0"]), p["a2"], p["g2"],
              vec(p["k_k"]), vec(p["k_a"]), vec(p["r_k"]), vec(p["lnx_g"]), vec(p["lnx_b"])]
    if has_vmix:
        params += [vec(p["v0"]), p["v1"], p["v2"]]
    args += params
    specs += [full(a) for a in params]

    out_spec = pl.BlockSpec((rows, CHUNK, W), tok)
    out_sds = jax.ShapeDtypeStruct((batch, seq, W), F32)
    outs = pl.pallas_call(
        functools.partial(_rwkv_kernel, has_vmix),
        grid=(batch // rows, nc),
        in_specs=specs,
        out_specs=out_spec if has_vmix else [out_spec, out_spec],
        out_shape=out_sds if has_vmix else [out_sds, out_sds],
        scratch_shapes=[pltpu.VMEM((rows, W // gw, HEAD_DIM, gw), F32)],
        compiler_params=pltpu.CompilerParams(
            dimension_semantics=("arbitrary", "arbitrary"), vmem_limit_bytes=VMEM_LIMIT),
        name="rwkv_vmix" if has_vmix else "rwkv",
    )(*args)
    if has_vmix:
        return outs.reshape(n, W), vfirst
    return outs[0].reshape(n, W), outs[1].reshape(n, W)


def _ffn_kernel(final, *refs):
    if final:
        (x_ref, att_ref, rw_ref, wo_ref, n2g_ref, wg_ref, wu_ref, cw_ref, cb_ref, wd_ref, fg_ref,
         out_ref, halo_ref) = refs
    else:
        (x_ref, att_ref, rw_ref, wo_ref, n2g_ref, wg_ref, wu_ref, cw_ref, cb_ref, wd_ref,
         out_ref, halo_ref) = refs
    tm = x_ref.shape[0]

    @pl.when(pl.program_id(1) == 0)
    def _():
        halo_ref[...] = jnp.zeros_like(halo_ref)

    x = (x_ref[...] + _dot(att_ref[...], wo_ref[:ATT_WIDTH, :])
         + _dot(rw_ref[...], wo_ref[ATT_WIDTH:, :]))
    h = _rms_norm(x, n2g_ref[...]).astype(BF16)
    u = jnp.dot(h, wg_ref[...], preferred_element_type=F32)
    up = jnp.dot(h, wu_ref[...], preferred_element_type=F32)
    ext = jnp.concatenate([halo_ref[...], u], axis=0)
    halo_ref[...] = u[tm - SUBLANES:, :]
    c = (cb_ref[...] + cw_ref[0:1, :] * ext[SUBLANES - 2:SUBLANES - 2 + tm, :]
         + cw_ref[1:2, :] * ext[SUBLANES - 1:SUBLANES - 1 + tm, :] + cw_ref[2:3, :] * u)
    act = (c * jax.nn.sigmoid(c)) * up
    x = x + _dot(act, wd_ref[...])
    if final:
        x = _rms_norm(x, fg_ref[...])
    out_ref[...] = x


def _ffn(xf, att, rw, wo, n2g, wg, wu, cw, cb, wd, fg, batch, seq, tm):
    n, d = xf.shape
    ff = wg.shape[1]
    nt = seq // tm
    tok = lambda b, t: (b * nt + t, 0)
    const = lambda b, t: (0, 0)
    resident = lambda shape: pl.BlockSpec(shape, const, pipeline_mode=pl.Buffered(1))
    final = fg is not None
    args = [xf, att, rw, wo, n2g.reshape(1, d), wg, wu, cw, cb.reshape(1, ff), wd]
    specs = [pl.BlockSpec((tm, d), tok),
             pl.BlockSpec((tm, ATT_WIDTH), tok),
             pl.BlockSpec((tm, RWKV_WIDTH), tok),
             resident(wo.shape), resident((1, d)), resident(wg.shape), resident(wu.shape),
             resident(cw.shape), resident((1, ff)), resident(wd.shape)]
    if final:
        args.append(fg.reshape(1, d))
        specs.append(resident((1, d)))
    return pl.pallas_call(
        functools.partial(_ffn_kernel, final),
        grid=(batch, nt),
        in_specs=specs,
        out_specs=pl.BlockSpec((tm, d), tok),
        out_shape=jax.ShapeDtypeStruct((n, d), F32),
        scratch_shapes=[pltpu.VMEM((SUBLANES, ff), F32)],
        compiler_params=pltpu.CompilerParams(
            dimension_semantics=("arbitrary", "arbitrary"), vmem_limit_bytes=VMEM_LIMIT),
        name="ffn_final" if final else "ffn",
    )(*args)


def kernel(x, norm1_g, w_in, attn_sinks, shift_mu, w0, w2, a0, a2, g2, k_k, k_a, r_k, lnx_g, lnx_b,
           v0, v1, v2, w_out, norm2_g, ffn_w_gate, ffn_w_up, conv_w, conv_b, ffn_w_down, final_g):
    batch, seq, d = x.shape
    depth = w_in.shape[0]
    assert seq % WINDOW == 0 and seq % CHUNK == 0
    tm_proj = min(512, seq)
    tm_ffn = min(256, seq)
    xf = x.reshape(batch * seq, d)
    vfirst = None
    for l in range(depth):
        att, rest = _proj_attn(xf, norm1_g[l], w_in[l].astype(BF16), shift_mu[l], attn_sinks[l], seq, tm_proj)
        p = dict(w0=w0[l], w2=w2[l], a0=a0[l], a2=a2[l], g2=g2[l],
                 k_k=k_k[l], k_a=k_a[l], r_k=r_k[l], lnx_g=lnx_g[l], lnx_b=lnx_b[l])
        if l > 0:
            p.update(v0=v0[l - 1], v1=v1[l - 1], v2=v2[l - 1])
        rw, vfirst = _rwkv(rest, vfirst, p, batch, seq)
        xf = _ffn(xf, att, rw, w_out[l].astype(BF16), norm2_g[l], ffn_w_gate[l].astype(BF16),
                  ffn_w_up[l].astype(BF16), conv_w[l], conv_b[l], ffn_w_down[l].astype(BF16),
                  final_g if l == depth - 1 else None, batch, seq, tm_ffn)
    return xf.reshape(batch, seq, d)
```

```python
import functools
import math

import jax
import jax.numpy as jnp
from jax import lax
from jax.experimental import pallas as pl
from jax.experimental.pallas import tpu as pltpu

F32 = jnp.float32
BF16 = jnp.bfloat16

HEAD_DIM = 64
ATT_HEADS = 8
ATT_KV_HEADS = 2
ATT_GROUP = ATT_HEADS // ATT_KV_HEADS
ATT_WIDTH = ATT_HEADS * HEAD_DIM
KV_WIDTH = ATT_KV_HEADS * HEAD_DIM
WINDOW = 128
RWKV_HEADS = 8
RWKV_WIDTH = RWKV_HEADS * HEAD_DIM
DECAY_LORA = 64
AAA_LORA = 64
GATE_LORA = 128
ATT_IN = ATT_WIDTH + 2 * KV_WIDTH
CONV_WIDTH = 3
RMS_EPS = 1e-5
LNX_EPS = 64e-5
DECAY_SCALE = math.exp(-0.5)
NEG_INF = -1e30

CHUNK = 64
INV_BASE = 8
SUBLANES = 8
RWKV_ROWS = 8
RWKV_ROWS_PER_PART = 4
CHUNK_DT = BF16
PREC_A = 1
PREC_INV = 1
PREC_APPLY = 1
PREC_KNORM = 1
PREC_SUM = 1
PREC_CUM = 2
GROUP = 4
VMEM_LIMIT = 56 * 1024 * 1024


def _dot(a, b):
    return jnp.dot(a.astype(BF16), b.astype(BF16), preferred_element_type=F32)


def _rms_norm(x, g):
    ms = jnp.mean(x * x, axis=-1, keepdims=True)
    return x * lax.rsqrt(ms + RMS_EPS) * g


def _proj_attn_kernel(tiles_per_seq, n_tiles, sink_ref, x_ref, g_ref, w_ref, mu_ref,
                      att_ref, shifted_ref, halo_ref, halo_used_ref, q_scr, kv_scr, q_stage, kv_stage):
    i = pl.program_id(0)
    tm = x_ref.shape[0]
    Wn, dh = WINDOW, HEAD_DIM
    nblk = tm // Wn

    @pl.when(i == 0)
    def _():
        q_stage[...] = jnp.zeros_like(q_stage)
        kv_stage[...] = jnp.zeros_like(kv_stage)
        halo_used_ref[...] = jnp.zeros_like(halo_used_ref)

    @pl.when(i % tiles_per_seq == 0)
    def _():
        halo_ref[...] = jnp.zeros_like(halo_ref)

    seq_start = (i + tiles_per_seq - 1) % tiles_per_seq == 0
    q_scr[...] = q_stage[...]
    kv_scr[...] = kv_stage[...]
    q_prev = q_scr[...]
    kv_prev = kv_scr[...]
    qpos = lax.broadcasted_iota(jnp.int32, (Wn, 2 * Wn), 0)
    kpos = lax.broadcasted_iota(jnp.int32, (Wn, 2 * Wn), 1)
    dist = qpos + Wn - kpos
    in_window = (dist >= 0) & (dist < Wn)
    distf = dist.astype(F32)
    lane = lax.broadcasted_iota(jnp.int32, (Wn, 2 * dh), 1)
    low = lane < dh
    scores = {}
    for b in range(nblk):
        for j in range(ATT_KV_HEADS):
            qs = jnp.concatenate([q_prev[b * Wn:(b + 1) * Wn, h * dh:(h + 1) * dh]
                                  for h in range(j * ATT_GROUP, (j + 1) * ATT_GROUP)], axis=0)
            k = kv_prev[b * Wn:(b + 2) * Wn, j * dh:(j + 1) * dh]
            scores[b, j] = lax.dot_general(qs, k, (((1,), (1,)), ((), ())), preferred_element_type=F32)

    hn = _rms_norm(x_ref[...], g_ref[...]).astype(BF16)
    q_new = jnp.dot(hn, w_ref[:, :ATT_WIDTH], preferred_element_type=F32)
    kv_new = jnp.dot(hn, w_ref[:, ATT_WIDTH:ATT_IN], preferred_element_type=F32)
    rest = jnp.dot(hn, w_ref[:, ATT_IN:], preferred_element_type=F32)
    halo = jnp.where(i == n_tiles, halo_used_ref[...], halo_ref[...])
    halo_used_ref[...] = halo
    trow = lax.broadcasted_iota(jnp.int32, rest.shape, 0)
    prev = jnp.where(trow == 0, halo[SUBLANES - 1:SUBLANES, :], pltpu.roll(rest, 1, 0))
    halo_ref[...] = rest[tm - SUBLANES:, :]
    shifted_ref[...] = rest + (prev - rest) * mu_ref[...]

    probs, sink_terms = {}, {}
    for b in range(nblk):
        valid = in_window & ((kpos >= Wn) | jnp.logical_not(seq_start)) if b == 0 else in_window
        for j in range(ATT_KV_HEADS):
            for g in range(ATT_GROUP):
                h = j * ATT_GROUP + g
                s = scores[b, j][g * Wn:(g + 1) * Wn, :] - (2.0 ** (-8.0 * (h + 1) / ATT_HEADS)) * distf
                s = jnp.where(valid, s, NEG_INF)
                sink = sink_ref[h]
                m = jnp.maximum(jnp.max(s, axis=-1, keepdims=True), sink)
                probs[b, h] = jnp.exp(s - m).astype(BF16)
                sink_terms[b, h] = jnp.exp(sink - m)
    v_even = [kv_prev[:, KV_WIDTH + 4 * j * dh:KV_WIDTH + (4 * j + 2) * dh] for j in range(ATT_KV_HEADS)]
    v_odd = [kv_prev[:, KV_WIDTH + (4 * j + 2) * dh:KV_WIDTH + (4 * j + 4) * dh] for j in range(ATT_KV_HEADS)]
    rows = []
    for b in range(nblk):
        cols = []
        for j in range(ATT_KV_HEADS):
            hs = range(j * ATT_GROUP, (j + 1) * ATT_GROUP)
            pe = jnp.concatenate([probs[b, h] for h in hs[0::2]], axis=0)
            po = jnp.concatenate([probs[b, h] for h in hs[1::2]], axis=0)
            oe = jnp.dot(pe, v_even[j][b * Wn:(b + 2) * Wn], preferred_element_type=F32)
            oo = jnp.dot(po, v_odd[j][b * Wn:(b + 2) * Wn], preferred_element_type=F32)
            for pair in range(ATT_GROUP // 2):
                e = oe[pair * Wn:(pair + 1) * Wn, :]
                o = oo[pair * Wn:(pair + 1) * Wn, :]
                num = jnp.where(low, e, o)
                den = pltpu.roll(jnp.where(low, o, e), dh, 1)
                den = den + jnp.where(low, sink_terms[b, hs[2 * pair]], sink_terms[b, hs[2 * pair + 1]])
                cols.append(num / den)
        rows.append(jnp.concatenate(cols, axis=1))
    att_ref[...] = jnp.concatenate(rows, axis=0)

    ones = jnp.ones((tm, dh), F32)
    pieces = [kv_new[:, :KV_WIDTH]]
    for j in range(ATT_KV_HEADS):
        v = kv_new[:, KV_WIDTH + j * dh:KV_WIDTH + (j + 1) * dh]
        pieces += [jnp.concatenate([v, ones], axis=1), jnp.concatenate([ones, v], axis=1)]
    kv_stage[0:Wn, :] = kv_prev[tm:tm + Wn, :]
    kv_stage[Wn:, :] = jnp.concatenate(pieces, axis=1).astype(BF16)
    q_stage[...] = (q_new * (dh ** -0.5)).astype(BF16)


def _proj_attn(xf, g, w, mu, sinks, seq, tm):
    n, d = xf.shape
    in_width = w.shape[1]
    shift_width = in_width - ATT_IN
    n_tiles = n // tm
    const = lambda i: (0, 0)
    cur = lambda i: (jnp.minimum(i, n_tiles - 1), 0)
    lag = lambda i: (jnp.maximum(i - 1, 0), 0)
    return pl.pallas_call(
        functools.partial(_proj_attn_kernel, seq // tm, n_tiles),
        grid=(n_tiles + 1,),
        in_specs=[pl.BlockSpec(memory_space=pltpu.SMEM),
                  pl.BlockSpec((tm, d), cur),
                  pl.BlockSpec((1, d), const),
                  pl.BlockSpec((d, in_width), const),
                  pl.BlockSpec((1, shift_width), const)],
        out_specs=[pl.BlockSpec((tm, ATT_WIDTH), lag),
                   pl.BlockSpec((tm, shift_width), cur)],
        out_shape=[jax.ShapeDtypeStruct((n, ATT_WIDTH), F32),
                   jax.ShapeDtypeStruct((n, shift_width), F32)],
        scratch_shapes=[pltpu.VMEM((SUBLANES, shift_width), F32),
                        pltpu.VMEM((SUBLANES, shift_width), F32),
                        pltpu.VMEM((tm, ATT_WIDTH), BF16),
                        pltpu.VMEM((WINDOW + tm, 5 * KV_WIDTH), BF16),
                        pltpu.VMEM((tm, ATT_WIDTH), BF16),
                        pltpu.VMEM((WINDOW + tm, 5 * KV_WIDTH), BF16)],
        compiler_params=pltpu.CompilerParams(
            dimension_semantics=("arbitrary",), vmem_limit_bytes=VMEM_LIMIT),
        name="proj_attn",
    )(sinks, xf, g.reshape(1, d), w, mu.reshape(1, shift_width))


def _split(x, n):
    parts, rem = [], x
    for i in range(n):
        p = rem.astype(CHUNK_DT)
        parts.append(p)
        if i + 1 < n:
            rem = rem - p.astype(F32)
    return parts


_DIMS = {"nn": (((1,), (0,)), ((), ())), "nt": (((1,), (1,)), ((), ())), "tn": (((0,), (0,)), ((), ()))}


def _pdot(ap, bp, dims="nn"):
    order = max(len(ap), len(bp))
    acc = None
    for i, a in enumerate(ap):
        for j, b in enumerate(bp):
            if i + j < order:
                t = lax.dot_general(a, b, _DIMS[dims], preferred_element_type=F32)
                acc = t if acc is None else acc + t
    return acc


def _gdot(x, y, bd_mask, prec, dims="nn"):
    ybd = jnp.where(bd_mask, jnp.concatenate([y] * GROUP, axis=0), 0.0)
    return _pdot(_split(x, prec), _split(ybd, prec), dims)


def _unit_lower_inverse(neg_as, rowi, colh, bd_mask, prec):
    C = CHUNK
    eye = (rowi == colh).astype(F32)
    ds = [jnp.where(rowi // INV_BASE == colh // INV_BASE, na, 0.0) for na in neg_as]
    ts = [eye + d for d in ds]
    ds = [_gdot(d, d, bd_mask, prec) for d in ds]
    span = 4
    while span < INV_BASE:
        xs = [_gdot(jnp.concatenate([t, d], axis=0), d, bd_mask, prec) for t, d in zip(ts, ds)]
        ts = [t + x[:C] for t, x in zip(ts, xs)]
        ds = [x[C:] for x in xs]
        span *= 2
    ts = [t + _gdot(t, d, bd_mask, prec) for t, d in zip(ts, ds)]
    size = INV_BASE
    while size < C:
        mask = (rowi // (2 * size) == colh // (2 * size)) & (rowi // size != colh // size)
        tos = [_gdot(t, jnp.where(mask, na, 0.0), bd_mask, prec) for t, na in zip(ts, neg_as)]
        ts = [t + _gdot(to, t, bd_mask, prec) for t, to in zip(ts, tos)]
        size *= 2
    return ts


def _rwkv_kernel(has_vmix, *refs):
    if has_vmix:
        (xs_ref, vfirst_ref, hs_ref, w0_ref, w2_ref, a0_ref, a2_ref, g2_ref, kk_ref,
         ka_ref, rk_ref, lng_ref, lnb_ref, v0_ref, v1_ref, v2_ref,
         out_ref, state_ref) = refs
    else:
        (xs_ref, hs_ref, w0_ref, w2_ref, a0_ref, a2_ref, g2_ref, kk_ref,
         ka_ref, rk_ref, lng_ref, lnb_ref, out_ref, vout_ref, state_ref) = refs
    C, W, N = CHUNK, RWKV_WIDTH, HEAD_DIM
    R = xs_ref.shape[0]

    @pl.when(pl.program_id(1) == 0)
    def _():
        state_ref[...] = jnp.zeros_like(state_ref)

    ones_bd = hs_ref[...]
    GW = ones_bd.shape[0]
    NG = W // GW
    bd_mask = (lax.broadcasted_iota(jnp.int32, (GW, GW), 0) // N
               == lax.broadcasted_iota(jnp.int32, (GW, GW), 1) // N)
    rowi = lax.broadcasted_iota(jnp.int32, (C, GW), 0)
    colh = lax.broadcasted_iota(jnp.int32, (C, GW), 1) % N
    strict = rowi > colh
    incl = rowi >= colh

    def head_sum(x, prec):
        n = x.shape[0]
        xs_ = jnp.concatenate([x[:, i * GW:(i + 1) * GW] for i in range(NG)], axis=0)
        s = _pdot(_split(xs_, prec), [ones_bd])
        return jnp.concatenate([s[i * n:(i + 1) * n] for i in range(NG)], axis=1)

    def prepare(b0, nb):
        n = nb * C
        xs = xs_ref[b0:b0 + nb].reshape(n, xs_ref.shape[2])
        r = xs[:, 0:W]
        k = xs[:, W:2 * W]
        v = xs[:, 2 * W:3 * W]
        wd = xs[:, 3 * W:3 * W + DECAY_LORA]
        ad = xs[:, 3 * W + DECAY_LORA:3 * W + DECAY_LORA + AAA_LORA]
        gd = xs[:, 3 * W + DECAY_LORA + AAA_LORA:]
        logd = -DECAY_SCALE * jax.nn.sigmoid(w0_ref[...] + _dot(jnp.tanh(wd), w2_ref[...]))
        a = jax.nn.sigmoid(a0_ref[...] + _dot(ad, a2_ref[...]))
        g = _dot(jax.nn.sigmoid(gd), g2_ref[...])
        kk = k * kk_ref[...]
        kk = kk * lax.rsqrt(jnp.maximum(head_sum(kk * kk, PREC_KNORM), 1e-24))
        k = k * (1.0 + (a - 1.0) * ka_ref[...])
        if has_vmix:
            gate = jax.nn.sigmoid(v0_ref[...] + _dot(_dot(v, v1_ref[...]), v2_ref[...]))
            v = v + (vfirst_ref[b0:b0 + nb].reshape(n, W) - v) * gate
        else:
            vout_ref[b0:b0 + nb] = v.reshape(nb, C, W)
        brow = lax.broadcasted_iota(jnp.int32, (n, n), 0)
        bcol = lax.broadcasted_iota(jnp.int32, (n, n), 1)
        tri = ((brow >= bcol) & (brow // C == bcol // C)).astype(CHUNK_DT)
        cum = _pdot([tri], _split(logd, PREC_CUM))
        grow = jnp.exp(-cum)
        kka = kk * a
        return dict(
            decay_end=[jnp.exp(cum[(b + 1) * C - 1:(b + 1) * C, :]) for b in range(nb)],
            r_t=r * jnp.exp(cum), a_t=-kk * jnp.exp(cum - logd), k_t=k * grow, b_t=kka * grow,
            v=v, g=g, rk=r * k * rk_ref[...])

    def recur(p, b0, nb, after_inverse=None):
        slabs = [(b, i) for b in range(nb) for i in range(NG)]

        def cut(x, b, i):
            return x[b * C:(b + 1) * C, i * GW:(i + 1) * GW]

        ar = [jnp.concatenate([cut(p["a_t"], b, i), cut(p["r_t"], b, i)], axis=0) for b, i in slabs]
        am_b = [_gdot(x, cut(p["b_t"], b, i), bd_mask, PREC_A, "nt") for x, (b, i) in zip(ar, slabs)]
        am_k = [_gdot(x, cut(p["k_t"], b, i), bd_mask, PREC_A, "nt") for x, (b, i) in zip(ar, slabs)]
        tinvs = _unit_lower_inverse([jnp.where(strict, m[:C], 0.0) for m in am_b], rowi, colh, bd_mask, PREC_INV)
        if after_inverse is not None:
            after_inverse()
        a_ak_rk =[jnp.concatenate([jnp.where(strict, m[:C], 0.0), jnp.where(incl, m[C:], 0.0)], axis=0)
                   for m in am_k]
        a_rb = [jnp.where(incl, m[C:], 0.0) for m in am_b]
        sts = [state_ref[b0 + b, i] for b, i in slabs]
        carry = [_gdot(x, st, bd_mask, PREC_APPLY, "nt") for x, st in zip(ar, sts)]
        intra = [_gdot(x, cut(p["v"], b, i), bd_mask, PREC_APPLY) for x, (b, i) in zip(a_ak_rk, slabs)]
        us = [_gdot(ti, cr[:C] + it[:C], bd_mask, PREC_APPLY)
              for ti, cr, it in zip(tinvs, carry, intra)]
        ys = [cr[C:] + it[C:] + _gdot(rb, u, bd_mask, PREC_APPLY)
              for cr, it, rb, u in zip(carry, intra, a_rb, us)]
        for (b, i), st, u in zip(slabs, sts, us):
            uv = jnp.concatenate([u, cut(p["v"], b, i)], axis=0)
            dend = p["decay_end"][b][:, i * GW:(i + 1) * GW]
            bk = jnp.concatenate([cut(p["b_t"], b, i), cut(p["k_t"], b, i)], axis=0) * dend
            full = _pdot(_split(uv, PREC_APPLY), _split(bk, PREC_APPLY), "tn")
            full = jnp.where(bd_mask, full, 0.0)
            st = dend * st
            for h in range(GROUP):
                st = st + full[h * N:(h + 1) * N, :]
            state_ref[b0 + b, i] = st
        return jnp.concatenate([jnp.concatenate(ys[b * NG:(b + 1) * NG], axis=1) for b in range(nb)], axis=0)

    def finish(p, y, b0, nb):
        mean = head_sum(y, PREC_SUM) * (1.0 / N)
        yc = y - mean
        var = head_sum(yc * yc, PREC_SUM) * (1.0 / N)
        y = yc * lax.rsqrt(var + LNX_EPS) * lng_ref[...] + lnb_ref[...]
        bonus = head_sum(p["rk"], PREC_SUM) * p["v"]
        out_ref[b0:b0 + nb] = ((y + bonus) * p["g"]).reshape(nb, C, W)

    parts = [(0, R // 2), (R // 2, R - R // 2)] if R >= 2 * RWKV_ROWS_PER_PART else [(0, R)]
    prepped = [prepare(*parts[0])]
    ys = []
    for n, (b0, nb) in enumerate(parts):
        nxt = (lambda m=n + 1: prepped.append(prepare(*parts[m]))) if n + 1 < len(parts) else None
        ys.append(recur(prepped[n], b0, nb, nxt))
    for p, y, (b0, nb) in zip(prepped, ys, parts):
        finish(p, y, b0, nb)


def _rwkv(rest, vfirst, p, batch, seq):
    n, shift_width = rest.shape
    nc = seq // CHUNK
    W = RWKV_WIDTH
    rows = RWKV_ROWS if batch % RWKV_ROWS == 0 else 1
    tok = lambda b, c: (b, c, 0)
    const = lambda b, c: (0, 0)
    vec = lambda a: a.reshape(1, -1)
    has_vmix = vfirst is not None
    full = lambda a: pl.BlockSpec(a.shape, const)
    gw = GROUP * HEAD_DIM
    hidx = jnp.arange(gw) // HEAD_DIM
    bd_mask = (hidx[:, None] == hidx[None, :]).astype(BF16)

    args = [rest.reshape(batch, seq, shift_width)]
    specs = [pl.BlockSpec((rows, CHUNK, shift_width), tok)]
    if has_vmix:
        args.append(vfirst.reshape(batch, seq, W))
        specs.append(pl.BlockSpec((rows, CHUNK, W), tok))
    params = [bd_mask, vec(p["w0"]), p["w2"], vec(p["a0"]), p["a2"], p["g2"],
              vec(p["k_k"]), vec(p["k_a"]), vec(p["r_k"]), vec(p["lnx_g"]), vec(p["lnx_b"])]
    if has_vmix:
        params += [vec(p["v0"]), p["v1"], p["v2"]]
    args += params
    specs += [full(a) for a in params]

    out_spec = pl.BlockSpec((rows, CHUNK, W), tok)
    out_sds = jax.ShapeDtypeStruct((batch, seq, W), F32)
    outs = pl.pallas_call(
        functools.partial(_rwkv_kernel, has_vmix),
        grid=(batch // rows, nc),
        in_specs=specs,
        out_specs=out_spec if has_vmix else [out_spec, out_spec],
        out_shape=out_sds if has_vmix else [out_sds, out_sds],
        scratch_shapes=[pltpu.VMEM((rows, W // gw, HEAD_DIM, gw), F32)],
        compiler_params=pltpu.CompilerParams(
            dimension_semantics=("arbitrary", "arbitrary"), vmem_limit_bytes=VMEM_LIMIT),
        name="rwkv_vmix" if has_vmix else "rwkv",
    )(*args)
    if has_vmix:
        return outs.reshape(n, W), vfirst
    return outs[0].reshape(n, W), outs[1].reshape(n, W)


def _ffn_kernel(final, *refs):
    if final:
        (x_ref, att_ref, rw_ref, wo_ref, n2g_ref, wg_ref, wu_ref, cw_ref, cb_ref, wd_ref, fg_ref,
         out_ref, halo_ref) = refs
    else:
        (x_ref, att_ref, rw_ref, wo_ref, n2g_ref, wg_ref, wu_ref, cw_ref, cb_ref, wd_ref,
         out_ref, halo_ref) = refs
    tm = x_ref.shape[0]

    @pl.when(pl.program_id(1) == 0)
    def _():
        halo_ref[...] = jnp.zeros_like(halo_ref)

    x = (x_ref[...] + _dot(att_ref[...], wo_ref[:ATT_WIDTH, :])
         + _dot(rw_ref[...], wo_ref[ATT_WIDTH:, :]))
    h = _rms_norm(x, n2g_ref[...]).astype(BF16)
    u = jnp.dot(h, wg_ref[...], preferred_element_type=F32)
    up = jnp.dot(h, wu_ref[...], preferred_element_type=F32)
    ext = jnp.concatenate([halo_ref[...], u], axis=0)
    halo_ref[...] = u[tm - SUBLANES:, :]
    c = (cb_ref[...] + cw_ref[0:1, :] * ext[SUBLANES - 2:SUBLANES - 2 + tm, :]
         + cw_ref[1:2, :] * ext[SUBLANES - 1:SUBLANES - 1 + tm, :] + cw_ref[2:3, :] * u)
    act = (c * jax.nn.sigmoid(c)) * up
    x = x + _dot(act, wd_ref[...])
    if final:
        x = _rms_norm(x, fg_ref[...])
    out_ref[...] = x


def _ffn(xf, att, rw, wo, n2g, wg, wu, cw, cb, wd, fg, batch, seq, tm):
    n, d = xf.shape
    ff = wg.shape[1]
    nt = seq // tm
    tok = lambda b, t: (b * nt + t, 0)
    const = lambda b, t: (0, 0)
    resident = lambda shape: pl.BlockSpec(shape, const, pipeline_mode=pl.Buffered(1))
    final = fg is not None
    args = [xf, att, rw, wo, n2g.reshape(1, d), wg, wu, cw, cb.reshape(1, ff), wd]
    specs = [pl.BlockSpec((tm, d), tok),
             pl.BlockSpec((tm, ATT_WIDTH), tok),
             pl.BlockSpec((tm, RWKV_WIDTH), tok),
             resident(wo.shape), resident((1, d)), resident(wg.shape), resident(wu.shape),
             resident(cw.shape), resident((1, ff)), resident(wd.shape)]
    if final:
        args.append(fg.reshape(1, d))
        specs.append(resident((1, d)))
    return pl.pallas_call(
        functools.partial(_ffn_kernel, final),
        grid=(batch, nt),
        in_specs=specs,
        out_specs=pl.BlockSpec((tm, d), tok),
        out_shape=jax.ShapeDtypeStruct((n, d), F32),
        scratch_shapes=[pltpu.VMEM((SUBLANES, ff), F32)],
        compiler_params=pltpu.CompilerParams(
            dimension_semantics=("arbitrary", "arbitrary"), vmem_limit_bytes=VMEM_LIMIT),
        name="ffn_final" if final else "ffn",
    )(*args)


def kernel(x, norm1_g, w_in, attn_sinks, shift_mu, w0, w2, a0, a2, g2, k_k, k_a, r_k, lnx_g, lnx_b,
           v0, v1, v2, w_out, norm2_g, ffn_w_gate, ffn_w_up, conv_w, conv_b, ffn_w_down, final_g):
    batch, seq, d = x.shape
    depth = w_in.shape[0]
    assert seq % WINDOW == 0 and seq % CHUNK == 0
    tm_proj = min(512, seq)
    tm_ffn = min(256, seq)
    xf = x.reshape(batch * seq, d)
    vfirst = None
    for l in range(depth):
        att, rest = _proj_attn(xf, norm1_g[l], w_in[l].astype(BF16), shift_mu[l], attn_sinks[l], seq, tm_proj)
        p = dict(w0=w0[l], w2=w2[l], a0=a0[l], a2=a2[l], g2=g2[l],
                 k_k=k_k[l], k_a=k_a[l], r_k=r_k[l], lnx_g=lnx_g[l], lnx_b=lnx_b[l])
        if l > 0:
            p.update(v0=v0[l - 1], v1=v1[l - 1], v2=v2[l - 1])
        rw, vfirst = _rwkv(rest, vfirst, p, batch, seq)
        xf = _ffn(xf, att, rw, w_out[l].astype(BF16), norm2_g[l], ffn_w_gate[l].astype(BF16),
                  ffn_w_up[l].astype(BF16), conv_w[l], conv_b[l], ffn_w_down[l].astype(BF16),
                  final_g if l == depth - 1 else None, batch, seq, tm_ffn)
    return xf.reshape(batch, seq, d)
```

```python
import functools
import math

import jax
import jax.numpy as jnp
from jax import lax
from jax.experimental import pallas as pl
from jax.experimental.pallas import tpu as pltpu

F32 = jnp.float32
BF16 = jnp.bfloat16

HEAD_DIM = 64
ATT_HEADS = 8
ATT_KV_HEADS = 2
ATT_GROUP = ATT_HEADS // ATT_KV_HEADS
ATT_WIDTH = ATT_HEADS * HEAD_DIM
KV_WIDTH = ATT_KV_HEADS * HEAD_DIM
WINDOW = 128
RWKV_HEADS = 8
RWKV_WIDTH = RWKV_HEADS * HEAD_DIM
DECAY_LORA = 64
AAA_LORA = 64
GATE_LORA = 128
ATT_IN = ATT_WIDTH + 2 * KV_WIDTH
CONV_WIDTH = 3
RMS_EPS = 1e-5
LNX_EPS = 64e-5
DECAY_SCALE = math.exp(-0.5)
NEG_INF = -1e30

CHUNK = 64
INV_BASE = 8
SUBLANES = 8
RWKV_ROWS = 8
RWKV_ROWS_PER_PART = 4
CHUNK_DT = BF16
PREC_A = 1
PREC_INV = 1
PREC_APPLY = 1
PREC_KNORM = 1
PREC_SUM = 1
PREC_CUM = 2
GROUP = 4
VMEM_LIMIT = 56 * 1024 * 1024


def _dot(a, b):
    return jnp.dot(a.astype(BF16), b.astype(BF16), preferred_element_type=F32)


def _rms_norm(x, g):
    ms = jnp.mean(x * x, axis=-1, keepdims=True)
    return x * lax.rsqrt(ms + RMS_EPS) * g


def _proj_attn_kernel(tiles_per_seq, n_tiles, sink_ref, x_ref, g_ref, w_ref, mu_ref,
                      att_ref, shifted_ref, halo_ref, halo_used_ref, q_scr, kv_scr, q_stage, kv_stage):
    i = pl.program_id(0)
    tm = x_ref.shape[0]
    Wn, dh = WINDOW, HEAD_DIM
    nblk = tm // Wn

    @pl.when(i == 0)
    def _():
        q_stage[...] = jnp.zeros_like(q_stage)
        kv_stage[...] = jnp.zeros_like(kv_stage)
        halo_used_ref[...] = jnp.zeros_like(halo_used_ref)

    @pl.when(i % tiles_per_seq == 0)
    def _():
        halo_ref[...] = jnp.zeros_like(halo_ref)

    seq_start = (i + tiles_per_seq - 1) % tiles_per_seq == 0
    q_scr[...] = q_stage[...]
    kv_scr[...] = kv_stage[...]
    q_prev = q_scr[...]
    kv_prev = kv_scr[...]
    qpos = lax.broadcasted_iota(jnp.int32, (Wn, 2 * Wn), 0)
    kpos = lax.broadcasted_iota(jnp.int32, (Wn, 2 * Wn), 1)
    dist = qpos + Wn - kpos
    in_window = (dist >= 0) & (dist < Wn)
    distf = dist.astype(F32)
    lane = lax.broadcasted_iota(jnp.int32, (Wn, 2 * dh), 1)
    low = lane < dh
    scores = {}
    for b in range(nblk):
        for j in range(ATT_KV_HEADS):
            qs = jnp.concatenate([q_prev[b * Wn:(b + 1) * Wn, h * dh:(h + 1) * dh]
                                  for h in range(j * ATT_GROUP, (j + 1) * ATT_GROUP)], axis=0)
            k = kv_prev[b * Wn:(b + 2) * Wn, j * dh:(j + 1) * dh]
            scores[b, j] = lax.dot_general(qs, k, (((1,), (1,)), ((), ())), preferred_element_type=F32)

    hn = _rms_norm(x_ref[...], g_ref[...]).astype(BF16)
    q_new = jnp.dot(hn, w_ref[:, :ATT_WIDTH], preferred_element_type=F32)
    kv_new = jnp.dot(hn, w_ref[:, ATT_WIDTH:ATT_IN], preferred_element_type=F32)
    rest = jnp.dot(hn, w_ref[:, ATT_IN:], preferred_element_type=F32)
    halo = jnp.where(i == n_tiles, halo_used_ref[...], halo_ref[...])
    halo_used_ref[...] = halo
    trow = lax.broadcasted_iota(jnp.int32, rest.shape, 0)
    prev = jnp.where(trow == 0, halo[SUBLANES - 1:SUBLANES, :], pltpu.roll(rest, 1, 0))
    halo_ref[...] = rest[tm - SUBLANES:, :]
    shifted_ref[...] = rest + (prev - rest) * mu_ref[...]

    probs, sink_terms = {}, {}
    for b in range(nblk):
        valid = in_window & ((kpos >= Wn) | jnp.logical_not(seq_start)) if b == 0 else in_window
        for j in range(ATT_KV_HEADS):
            for g in range(ATT_GROUP):
                h = j * ATT_GROUP + g
                s = scores[b, j][g * Wn:(g + 1) * Wn, :] - (2.0 ** (-8.0 * (h + 1) / ATT_HEADS)) * distf
                s = jnp.where(valid, s, NEG_INF)
                sink = sink_ref[h]
                m = jnp.maximum(jnp.max(s, axis=-1, keepdims=True), sink)
                probs[b, h] = jnp.exp(s - m).astype(BF16)
                sink_terms[b, h] = jnp.exp(sink - m)
    v_even = [kv_prev[:, KV_WIDTH + 4 * j * dh:KV_WIDTH + (4 * j + 2) * dh] for j in range(ATT_KV_HEADS)]
    v_odd = [kv_prev[:, KV_WIDTH + (4 * j + 2) * dh:KV_WIDTH + (4 * j + 4) * dh] for j in range(ATT_KV_HEADS)]
    rows = []
    for b in range(nblk):
        cols = []
        for j in range(ATT_KV_HEADS):
            hs = range(j * ATT_GROUP, (j + 1) * ATT_GROUP)
            pe = jnp.concatenate([probs[b, h] for h in hs[0::2]], axis=0)
            po = jnp.concatenate([probs[b, h] for h in hs[1::2]], axis=0)
            oe = jnp.dot(pe, v_even[j][b * Wn:(b + 2) * Wn], preferred_element_type=F32)
            oo = jnp.dot(po, v_odd[j][b * Wn:(b + 2) * Wn], preferred_element_type=F32)
            for pair in range(ATT_GROUP // 2):
                e = oe[pair * Wn:(pair + 1) * Wn, :]
                o = oo[pair * Wn:(pair + 1) * Wn, :]
                num = jnp.where(low, e, o)
                den = pltpu.roll(jnp.where(low, o, e), dh, 1)
                den = den + jnp.where(low, sink_terms[b, hs[2 * pair]], sink_terms[b, hs[2 * pair + 1]])
                cols.append(num / den)
        rows.append(jnp.concatenate(cols, axis=1))
    att_ref[...] = jnp.concatenate(rows, axis=0)

    ones = jnp.ones((tm, dh), F32)
    pieces = [kv_new[:, :KV_WIDTH]]
    for j in range(ATT_KV_HEADS):
        v = kv_new[:, KV_WIDTH + j * dh:KV_WIDTH + (j + 1) * dh]
        pieces += [jnp.concatenate([v, ones], axis=1), jnp.concatenate([ones, v], axis=1)]
    kv_stage[0:Wn, :] = kv_prev[tm:tm + Wn, :]
    kv_stage[Wn:, :] = jnp.concatenate(pieces, axis=1).astype(BF16)
    q_stage[...] = (q_new * (dh ** -0.5)).astype(BF16)


def _proj_attn(xf, g, w, mu, sinks, seq, tm):
    n, d = xf.shape
    in_width = w.shape[1]
    shift_width = in_width - ATT_IN
    n_tiles = n // tm
    const = lambda i: (0, 0)
    cur = lambda i: (jnp.minimum(i, n_tiles - 1), 0)
    lag = lambda i: (jnp.maximum(i - 1, 0), 0)
    return pl.pallas_call(
        functools.partial(_proj_attn_kernel, seq // tm, n_tiles),
        grid=(n_tiles + 1,),
        in_specs=[pl.BlockSpec(memory_space=pltpu.SMEM),
                  pl.BlockSpec((tm, d), cur),
                  pl.BlockSpec((1, d), const),
                  pl.BlockSpec((d, in_width), const),
                  pl.BlockSpec((1, shift_width), const)],
        out_specs=[pl.BlockSpec((tm, ATT_WIDTH), lag),
                   pl.BlockSpec((tm, shift_width), cur)],
        out_shape=[jax.ShapeDtypeStruct((n, ATT_WIDTH), F32),
                   jax.ShapeDtypeStruct((n, shift_width), F32)],
        scratch_shapes=[pltpu.VMEM((SUBLANES, shift_width), F32),
                        pltpu.VMEM((SUBLANES, shift_width), F32),
                        pltpu.VMEM((tm, ATT_WIDTH), BF16),
                        pltpu.VMEM((WINDOW + tm, 5 * KV_WIDTH), BF16),
                        pltpu.VMEM((tm, ATT_WIDTH), BF16),
                        pltpu.VMEM((WINDOW + tm, 5 * KV_WIDTH), BF16)],
        compiler_params=pltpu.CompilerParams(
            dimension_semantics=("arbitrary",), vmem_limit_bytes=VMEM_LIMIT),
        name="proj_attn",
    )(sinks, xf, g.reshape(1, d), w, mu.reshape(1, shift_width))


def _split(x, n):
    parts, rem = [], x
    for i in range(n):
        p = rem.astype(CHUNK_DT)
        parts.append(p)
        if i + 1 < n:
            rem = rem - p.astype(F32)
    return parts


_DIMS = {"nn": (((1,), (0,)), ((), ())), "nt": (((1,), (1,)), ((), ())), "tn": (((0,), (0,)), ((), ()))}


def _pdot(ap, bp, dims="nn"):
    order = max(len(ap), len(bp))
    acc = None
    for i, a in enumerate(ap):
        for j, b in enumerate(bp):
            if i + j < order:
                t = lax.dot_general(a, b, _DIMS[dims], preferred_element_type=F32)
                acc = t if acc is None else acc + t
    return acc


def _gdot(x, y, bd_mask, prec, dims="nn"):
    ybd = jnp.where(bd_mask, jnp.concatenate([y] * GROUP, axis=0), 0.0)
    return _pdot(_split(x, prec), _split(ybd, prec), dims)


def _unit_lower_inverse(neg_as, rowi, colh, bd_mask, prec):
    C = CHUNK
    eye = (rowi == colh).astype(F32)
    ds = [jnp.where(rowi // INV_BASE == colh // INV_BASE, na, 0.0) for na in neg_as]
    ts = [eye + d for d in ds]
    ds = [_gdot(d, d, bd_mask, prec) for d in ds]
    span = 4
    while span < INV_BASE:
        xs = [_gdot(jnp.concatenate([t, d], axis=0), d, bd_mask, prec) for t, d in zip(ts, ds)]
        ts = [t + x[:C] for t, x in zip(ts, xs)]
        ds = [x[C:] for x in xs]
        span *= 2
    ts = [t + _gdot(t, d, bd_mask, prec) for t, d in zip(ts, ds)]
    size = INV_BASE
    while size < C:
        mask = (rowi // (2 * size) == colh // (2 * size)) & (rowi // size != colh // size)
        tos = [_gdot(t, jnp.where(mask, na, 0.0), bd_mask, prec) for t, na in zip(ts, neg_as)]
        ts = [t + _gdot(to, t, bd_mask, prec) for t, to in zip(ts, tos)]
        size *= 2
    return ts


def _rwkv_kernel(has_vmix, *refs):
    if has_vmix:
        (xs_ref, vfirst_ref, hs_ref, w0_ref, w2_ref, a0_ref, a2_ref, g2_ref, kk_ref,
         ka_ref, rk_ref, lng_ref, lnb_ref, v0_ref, v1_ref, v2_ref,
         out_ref, state_ref) = refs
    else:
        (xs_ref, hs_ref, w0_ref, w2_ref, a0_ref, a2_ref, g2_ref, kk_ref,
         ka_ref, rk_ref, lng_ref, lnb_ref, out_ref, vout_ref, state_ref) = refs
    C, W, N = CHUNK, RWKV_WIDTH, HEAD_DIM
    R = xs_ref.shape[0]

    @pl.when(pl.program_id(1) == 0)
    def _():
        state_ref[...] = jnp.zeros_like(state_ref)

    ones_bd = hs_ref[...]
    GW = ones_bd.shape[0]
    NG = W // GW
    bd_mask = (lax.broadcasted_iota(jnp.int32, (GW, GW), 0) // N
               == lax.broadcasted_iota(jnp.int32, (GW, GW), 1) // N)
    rowi = lax.broadcasted_iota(jnp.int32, (C, GW), 0)
    colh = lax.broadcasted_iota(jnp.int32, (C, GW), 1) % N
    strict = rowi > colh
    incl = rowi >= colh

    def head_sum(x, prec):
        n = x.shape[0]
        xs_ = jnp.concatenate([x[:, i * GW:(i + 1) * GW] for i in range(NG)], axis=0)
        s = _pdot(_split(xs_, prec), [ones_bd])
        return jnp.concatenate([s[i * n:(i + 1) * n] for i in range(NG)], axis=1)

    def prepare(b0, nb):
        n = nb * C
        xs = xs_ref[b0:b0 + nb].reshape(n, xs_ref.shape[2])
        r = xs[:, 0:W]
        k = xs[:, W:2 * W]
        v = xs[:, 2 * W:3 * W]
        wd = xs[:, 3 * W:3 * W + DECAY_LORA]
        ad = xs[:, 3 * W + DECAY_LORA:3 * W + DECAY_LORA + AAA_LORA]
        gd = xs[:, 3 * W + DECAY_LORA + AAA_LORA:]
        logd = -DECAY_SCALE * jax.nn.sigmoid(w0_ref[...] + _dot(jnp.tanh(wd), w2_ref[...]))
        a = jax.nn.sigmoid(a0_ref[...] + _dot(ad, a2_ref[...]))
        g = _dot(jax.nn.sigmoid(gd), g2_ref[...])
        kk = k * kk_ref[...]
        kk = kk * lax.rsqrt(jnp.maximum(head_sum(kk * kk, PREC_KNORM), 1e-24))
        k = k * (1.0 + (a - 1.0) * ka_ref[...])
        if has_vmix:
            gate = jax.nn.sigmoid(v0_ref[...] + _dot(_dot(v, v1_ref[...]), v2_ref[...]))
            v = v + (vfirst_ref[b0:b0 + nb].reshape(n, W) - v) * gate
        else:
            vout_ref[b0:b0 + nb] = v.reshape(nb, C, W)
        brow = lax.broadcasted_iota(jnp.int32, (n, n), 0)
        bcol = lax.broadcasted_iota(jnp.int32, (n, n), 1)
        tri = ((brow >= bcol) & (brow // C == bcol // C)).astype(CHUNK_DT)
        cum = _pdot([tri], _split(logd, PREC_CUM))
        grow = jnp.exp(-cum)
        kka = kk * a
        return dict(
            decay_end=[jnp.exp(cum[(b + 1) * C - 1:(b + 1) * C, :]) for b in range(nb)],
            r_t=r * jnp.exp(cum), a_t=-kk * jnp.exp(cum - logd), k_t=k * grow, b_t=kka * grow,
            v=v, g=g, rk=r * k * rk_ref[...])

    def recur(p, b0, nb, after_inverse=None):
        slabs = [(b, i) for b in range(nb) for i in range(NG)]

        def cut(x, b, i):
            return x[b * C:(b + 1) * C, i * GW:(i + 1) * GW]

        ar = [jnp.concatenate([cut(p["a_t"], b, i), cut(p["r_t"], b, i)], axis=0) for b, i in slabs]
        am_b = [_gdot(x, cut(p["b_t"], b, i), bd_mask, PREC_A, "nt") for x, (b, i) in zip(ar, slabs)]
        am_k = [_gdot(x, cut(p["k_t"], b, i), bd_mask, PREC_A, "nt") for x, (b, i) in zip(ar, slabs)]
        tinvs = _unit_lower_inverse([jnp.where(strict, m[:C], 0.0) for m in am_b], rowi, colh, bd_mask, PREC_INV)
        if after_inverse is not None:
            after_inverse()
        a_ak_rk =[jnp.concatenate([jnp.where(strict, m[:C], 0.0), jnp.where(incl, m[C:], 0.0)], axis=0)
                   for m in am_k]
        a_rb = [jnp.where(incl, m[C:], 0.0) for m in am_b]
        sts = [state_ref[b0 + b, i] for b, i in slabs]
        carry = [_gdot(x, st, bd_mask, PREC_APPLY, "nt") for x, st in zip(ar, sts)]
        intra = [_gdot(x, cut(p["v"], b, i), bd_mask, PREC_APPLY) for x, (b, i) in zip(a_ak_rk, slabs)]
        us = [_gdot(ti, cr[:C] + it[:C], bd_mask, PREC_APPLY)
              for ti, cr, it in zip(tinvs, carry, intra)]
        ys = [cr[C:] + it[C:] + _gdot(rb, u, bd_mask, PREC_APPLY)
              for cr, it, rb, u in zip(carry, intra, a_rb, us)]
        for (b, i), st, u in zip(slabs, sts, us):
            uv = jnp.concatenate([u, cut(p["v"], b, i)], axis=0)
            dend = p["decay_end"][b][:, i * GW:(i + 1) * GW]
            bk = jnp.concatenate([cut(p["b_t"], b, i), cut(p["k_t"], b, i)], axis=0) * dend
            full = _pdot(_split(uv, PREC_APPLY), _split(bk, PREC_APPLY), "tn")
            full = jnp.where(bd_mask, full, 0.0)
            st = dend * st
            for h in range(GROUP):
                st = st + full[h * N:(h + 1) * N, :]
            state_ref[b0 + b, i] = st
        return jnp.concatenate([jnp.concatenate(ys[b * NG:(b + 1) * NG], axis=1) for b in range(nb)], axis=0)

    def finish(p, y, b0, nb):
        mean = head_sum(y, PREC_SUM) * (1.0 / N)
        yc = y - mean
        var = head_sum(yc * yc, PREC_SUM) * (1.0 / N)
        y = yc * lax.rsqrt(var + LNX_EPS) * lng_ref[...] + lnb_ref[...]
        bonus = head_sum(p["rk"], PREC_SUM) * p["v"]
        out_ref[b0:b0 + nb] = ((y + bonus) * p["g"]).reshape(nb, C, W)

    parts = [(0, R // 2), (R // 2, R - R // 2)] if R >= 2 * RWKV_ROWS_PER_PART else [(0, R)]
    prepped = [prepare(*parts[0])]
    ys = []
    for n, (b0, nb) in enumerate(parts):
        nxt = (lambda m=n + 1: prepped.append(prepare(*parts[m]))) if n + 1 < len(parts) else None
        ys.append(recur(prepped[n], b0, nb, nxt))
    for p, y, (b0, nb) in zip(prepped, ys, parts):
        finish(p, y, b0, nb)


def _rwkv(rest, vfirst, p, batch, seq):
    n, shift_width = rest.shape
    nc = seq // CHUNK
    W = RWKV_WIDTH
    rows = RWKV_ROWS if batch % RWKV_ROWS == 0 else 1
    tok = lambda b, c: (b, c, 0)
    const = lambda b, c: (0, 0)
    vec = lambda a: a.reshape(1, -1)
    has_vmix = vfirst is not None
    full = lambda a: pl.BlockSpec(a.shape, const)
    gw = GROUP * HEAD_DIM
    hidx = jnp.arange(gw) // HEAD_DIM
    bd_mask = (hidx[:, None] == hidx[None, :]).astype(BF16)

    args = [rest.reshape(batch, seq, shift_width)]
    specs = [pl.BlockSpec((rows, CHUNK, shift_width), tok)]
    if has_vmix:
        args.append(vfirst.reshape(batch, seq, W))
        specs.append(pl.BlockSpec((rows, CHUNK, W), tok))
    params = [bd_mask, vec(p["w0"]), p["w2"], vec(p["a0"]), p["a2"], p["g2"],
              vec(p["k_k"]), vec(p["k_a"]), vec(p["r_k"]), vec(p["lnx_g"]), vec(p["lnx_b"])]
    if has_vmix:
        params += [vec(p["v0"]), p["v1"], p["v2"]]
    args += params
    specs += [full(a) for a in params]

    out_spec = pl.BlockSpec((rows, CHUNK, W), tok)
    out_sds = jax.ShapeDtypeStruct((batch, seq, W), F32)
    outs = pl.pallas_call(
        functools.partial(_rwkv_kernel, has_vmix),
        grid=(batch // rows, nc),
        in_specs=specs,
        out_specs=out_spec if has_vmix else [out_spec, out_spec],
        out_shape=out_sds if has_vmix else [out_sds, out_sds],
        scratch_shapes=[pltpu.VMEM((rows, W // gw, HEAD_DIM, gw), F32)],
        compiler_params=pltpu.CompilerParams(
            dimension_semantics=("arbitrary", "arbitrary"), vmem_limit_bytes=VMEM_LIMIT),
        name="rwkv_vmix" if has_vmix else "rwkv",
    )(*args)
    if has_vmix:
        return outs.reshape(n, W), vfirst
    return outs[0].reshape(n, W), outs[1].reshape(n, W)


def _ffn_kernel(final, *refs):
    if final:
        (x_ref, att_ref, rw_ref, wo_ref, n2g_ref, wg_ref, wu_ref, cw_ref, cb_ref, wd_ref, fg_ref,
         out_ref, halo_ref) = refs
    else:
        (x_ref, att_ref, rw_ref, wo_ref, n2g_ref, wg_ref, wu_ref, cw_ref, cb_ref, wd_ref,
         out_ref, halo_ref) = refs
    tm = x_ref.shape[0]

    @pl.when(pl.program_id(1) == 0)
    def _():
        halo_ref[...] = jnp.zeros_like(halo_ref)

    x = (x_ref[...] + _dot(att_ref[...], wo_ref[:ATT_WIDTH, :])
         + _dot(rw_ref[...], wo_ref[ATT_WIDTH:, :]))
    h = _rms_norm(x, n2g_ref[...]).astype(BF16)
    u = jnp.dot(h, wg_ref[...], preferred_element_type=F32)
    up = jnp.dot(h, wu_ref[...], preferred_element_type=F32)
    ext = jnp.concatenate([halo_ref[...], u], axis=0)
    halo_ref[...] = u[tm - SUBLANES:, :]
    c = (cb_ref[...] + cw_ref[0:1, :] * ext[SUBLANES - 2:SUBLANES - 2 + tm, :]
         + cw_ref[1:2, :] * ext[SUBLANES - 1:SUBLANES - 1 + tm, :] + cw_ref[2:3, :] * u)
    act = (c * jax.nn.sigmoid(c)) * up
    x = x + _dot(act, wd_ref[...])
    if final:
        x = _rms_norm(x, fg_ref[...])
    out_ref[...] = x


def _ffn(xf, att, rw, wo, n2g, wg, wu, cw, cb, wd, fg, batch, seq, tm):
    n, d = xf.shape
    ff = wg.shape[1]
    nt = seq // tm
    tok = lambda b, t: (b * nt + t, 0)
    const = lambda b, t: (0, 0)
    resident = lambda shape: pl.BlockSpec(shape, const, pipeline_mode=pl.Buffered(1))
    final = fg is not None
    args = [xf, att, rw, wo, n2g.reshape(1, d), wg, wu, cw, cb.reshape(1, ff), wd]
    specs = [pl.BlockSpec((tm, d), tok),
             pl.BlockSpec((tm, ATT_WIDTH), tok),
             pl.BlockSpec((tm, RWKV_WIDTH), tok),
             resident(wo.shape), resident((1, d)), resident(wg.shape), resident(wu.shape),
             resident(cw.shape), resident((1, ff)), resident(wd.shape)]
    if final:
        args.append(fg.reshape(1, d))
        specs.append(resident((1, d)))
    return pl.pallas_call(
        functools.partial(_ffn_kernel, final),
        grid=(batch, nt),
        in_specs=specs,
        out_specs=pl.BlockSpec((tm, d), tok),
        out_shape=jax.ShapeDtypeStruct((n, d), F32),
        scratch_shapes=[pltpu.VMEM((SUBLANES, ff), F32)],
        compiler_params=pltpu.CompilerParams(
            dimension_semantics=("arbitrary", "arbitrary"), vmem_limit_bytes=VMEM_LIMIT),
        name="ffn_final" if final else "ffn",
    )(*args)


def kernel(x, norm1_g, w_in, attn_sinks, shift_mu, w0, w2, a0, a2, g2, k_k, k_a, r_k, lnx_g, lnx_b,
           v0, v1, v2, w_out, norm2_g, ffn_w_gate, ffn_w_up, conv_w, conv_b, ffn_w_down, final_g):
    batch, seq, d = x.shape
    depth = w_in.shape[0]
    assert seq % WINDOW == 0 and seq % CHUNK == 0
    tm_proj = min(512, seq)
    tm_ffn = min(512, seq)
    xf = x.reshape(batch * seq, d)
    vfirst = None
    for l in range(depth):
        att, rest = _proj_attn(xf, norm1_g[l], w_in[l].astype(BF16), shift_mu[l], attn_sinks[l], seq, tm_proj)
        p = dict(w0=w0[l], w2=w2[l], a0=a0[l], a2=a2[l], g2=g2[l],
                 k_k=k_k[l], k_a=k_a[l], r_k=r_k[l], lnx_g=lnx_g[l], lnx_b=lnx_b[l])
        if l > 0:
            p.update(v0=v0[l - 1], v1=v1[l - 1], v2=v2[l - 1])
        rw, vfirst = _rwkv(rest, vfirst, p, batch, seq)
        xf = _ffn(xf, att, rw, w_out[l].astype(BF16), norm2_g[l], ffn_w_gate[l].astype(BF16),
                  ffn_w_up[l].astype(BF16), conv_w[l], conv_b[l], ffn_w_down[l].astype(BF16),
                  final_g if l == depth - 1 else None, batch, seq, tm_ffn)
    return xf.reshape(batch, seq, d)
```

```python
import functools
import math

import jax
import jax.numpy as jnp
from jax import lax
from jax.experimental import pallas as pl
from jax.experimental.pallas import tpu as pltpu

F32 = jnp.float32
BF16 = jnp.bfloat16

HEAD_DIM = 64
ATT_HEADS = 8
ATT_KV_HEADS = 2
ATT_GROUP = ATT_HEADS // ATT_KV_HEADS
ATT_WIDTH = ATT_HEADS * HEAD_DIM
KV_WIDTH = ATT_KV_HEADS * HEAD_DIM
WINDOW = 128
RWKV_HEADS = 8
RWKV_WIDTH = RWKV_HEADS * HEAD_DIM
DECAY_LORA = 64
AAA_LORA = 64
GATE_LORA = 128
ATT_IN = ATT_WIDTH + 2 * KV_WIDTH
CONV_WIDTH = 3
RMS_EPS = 1e-5
LNX_EPS = 64e-5
DECAY_SCALE = math.exp(-0.5)
NEG_INF = -1e30

CHUNK = 64
INV_BASE = 8
SUBLANES = 8
RWKV_ROWS = 8
RWKV_ROWS_PER_PART = 4
CHUNK_DT = BF16
PREC_A = 1
PREC_INV = 1
PREC_APPLY = 1
PREC_KNORM = 1
PREC_SUM = 1
PREC_CUM = 2
GROUP = 4
KNORM_FLOOR = 1e-12
ROW_TILE = 512
VMEM_LIMIT = 56 * 1024 * 1024


def _dot(a, b):
    return jnp.dot(a.astype(BF16), b.astype(BF16), preferred_element_type=F32)


def _rms_norm(x, g):
    ms = jnp.mean(x * x, axis=-1, keepdims=True)
    return x * lax.rsqrt(ms + RMS_EPS) * g


def _proj_attn_kernel(tiles_per_seq, n_tiles, sink_ref, x_ref, g_ref, w_ref, mu_ref,
                      att_ref, shifted_ref, halo_ref, halo_used_ref, q_scr, kv_scr, q_stage, kv_stage):
    i = pl.program_id(0)
    tm = x_ref.shape[0]
    Wn, dh = WINDOW, HEAD_DIM
    nblk = tm // Wn

    @pl.when(i == 0)
    def _():
        q_stage[...] = jnp.zeros_like(q_stage)
        kv_stage[...] = jnp.zeros_like(kv_stage)
        halo_used_ref[...] = jnp.zeros_like(halo_used_ref)

    @pl.when(i % tiles_per_seq == 0)
    def _():
        halo_ref[...] = jnp.zeros_like(halo_ref)

    seq_start = (i + tiles_per_seq - 1) % tiles_per_seq == 0
    q_scr[...] = q_stage[...]
    kv_scr[...] = kv_stage[...]
    q_prev = q_scr[...]
    kv_prev = kv_scr[...]
    qpos = lax.broadcasted_iota(jnp.int32, (Wn, 2 * Wn), 0)
    kpos = lax.broadcasted_iota(jnp.int32, (Wn, 2 * Wn), 1)
    dist = qpos + Wn - kpos
    in_window = (dist >= 0) & (dist < Wn)
    distf = dist.astype(F32)
    lane = lax.broadcasted_iota(jnp.int32, (Wn, 2 * dh), 1)
    low = lane < dh
    scores = {}
    for b in range(nblk):
        for j in range(ATT_KV_HEADS):
            qs = jnp.concatenate([q_prev[b * Wn:(b + 1) * Wn, h * dh:(h + 1) * dh]
                                  for h in range(j * ATT_GROUP, (j + 1) * ATT_GROUP)], axis=0)
            k = kv_prev[b * Wn:(b + 2) * Wn, j * dh:(j + 1) * dh]
            scores[b, j] = lax.dot_general(qs, k, (((1,), (1,)), ((), ())), preferred_element_type=F32)

    hn = _rms_norm(x_ref[...], g_ref[...]).astype(BF16)
    q_new = jnp.dot(hn, w_ref[:, :ATT_WIDTH], preferred_element_type=F32)
    kv_new = jnp.dot(hn, w_ref[:, ATT_WIDTH:ATT_IN], preferred_element_type=F32)
    rest = jnp.dot(hn, w_ref[:, ATT_IN:], preferred_element_type=F32)
    halo = jnp.where(i == n_tiles, halo_used_ref[...], halo_ref[...])
    halo_used_ref[...] = halo
    trow = lax.broadcasted_iota(jnp.int32, rest.shape, 0)
    prev = jnp.where(trow == 0, halo[SUBLANES - 1:SUBLANES, :], pltpu.roll(rest, 1, 0))
    halo_ref[...] = rest[tm - SUBLANES:, :]
    shifted_ref[...] = rest + (prev - rest) * mu_ref[...]

    probs, sink_terms = {}, {}
    for b in range(nblk):
        valid = in_window & ((kpos >= Wn) | jnp.logical_not(seq_start)) if b == 0 else in_window
        for j in range(ATT_KV_HEADS):
            for g in range(ATT_GROUP):
                h = j * ATT_GROUP + g
                s = scores[b, j][g * Wn:(g + 1) * Wn, :] - (2.0 ** (-8.0 * (h + 1) / ATT_HEADS)) * distf
                s = jnp.where(valid, s, NEG_INF)
                sink = sink_ref[h]
                m = jnp.maximum(jnp.max(s, axis=-1, keepdims=True), sink)
                probs[b, h] = jnp.exp(s - m).astype(BF16)
                sink_terms[b, h] = jnp.exp(sink - m)
    v_even = [kv_prev[:, KV_WIDTH + 4 * j * dh:KV_WIDTH + (4 * j + 2) * dh] for j in range(ATT_KV_HEADS)]
    v_odd = [kv_prev[:, KV_WIDTH + (4 * j + 2) * dh:KV_WIDTH + (4 * j + 4) * dh] for j in range(ATT_KV_HEADS)]
    rows = []
    for b in range(nblk):
        cols = []
        for j in range(ATT_KV_HEADS):
            hs = range(j * ATT_GROUP, (j + 1) * ATT_GROUP)
            pe = jnp.concatenate([probs[b, h] for h in hs[0::2]], axis=0)
            po = jnp.concatenate([probs[b, h] for h in hs[1::2]], axis=0)
            oe = jnp.dot(pe, v_even[j][b * Wn:(b + 2) * Wn], preferred_element_type=F32)
            oo = jnp.dot(po, v_odd[j][b * Wn:(b + 2) * Wn], preferred_element_type=F32)
            for pair in range(ATT_GROUP // 2):
                e = oe[pair * Wn:(pair + 1) * Wn, :]
                o = oo[pair * Wn:(pair + 1) * Wn, :]
                num = jnp.where(low, e, o)
                den = pltpu.roll(jnp.where(low, o, e), dh, 1)
                den = den + jnp.where(low, sink_terms[b, hs[2 * pair]], sink_terms[b, hs[2 * pair + 1]])
                cols.append(num / den)
        rows.append(jnp.concatenate(cols, axis=1))
    att_ref[...] = jnp.concatenate(rows, axis=0)

    ones = jnp.ones((tm, dh), F32)
    pieces = [kv_new[:, :KV_WIDTH]]
    for j in range(ATT_KV_HEADS):
        v = kv_new[:, KV_WIDTH + j * dh:KV_WIDTH + (j + 1) * dh]
        pieces += [jnp.concatenate([v, ones], axis=1), jnp.concatenate([ones, v], axis=1)]
    kv_stage[0:Wn, :] = kv_prev[tm:tm + Wn, :]
    kv_stage[Wn:, :] = jnp.concatenate(pieces, axis=1).astype(BF16)
    q_stage[...] = (q_new * (dh ** -0.5)).astype(BF16)


def _proj_attn(xf, g, w, mu, sinks, seq, tm):
    n, d = xf.shape
    in_width = w.shape[1]
    shift_width = in_width - ATT_IN
    n_tiles = n // tm
    const = lambda i: (0, 0)
    cur = lambda i: (jnp.minimum(i, n_tiles - 1), 0)
    lag = lambda i: (jnp.maximum(i - 1, 0), 0)
    return pl.pallas_call(
        functools.partial(_proj_attn_kernel, seq // tm, n_tiles),
        grid=(n_tiles + 1,),
        in_specs=[pl.BlockSpec(memory_space=pltpu.SMEM),
                  pl.BlockSpec((tm, d), cur),
                  pl.BlockSpec((1, d), const),
                  pl.BlockSpec((d, in_width), const),
                  pl.BlockSpec((1, shift_width), const)],
        out_specs=[pl.BlockSpec((tm, ATT_WIDTH), lag),
                   pl.BlockSpec((tm, shift_width), cur)],
        out_shape=[jax.ShapeDtypeStruct((n, ATT_WIDTH), F32),
                   jax.ShapeDtypeStruct((n, shift_width), F32)],
        scratch_shapes=[pltpu.VMEM((SUBLANES, shift_width), F32),
                        pltpu.VMEM((SUBLANES, shift_width), F32),
                        pltpu.VMEM((tm, ATT_WIDTH), BF16),
                        pltpu.VMEM((WINDOW + tm, 5 * KV_WIDTH), BF16),
                        pltpu.VMEM((tm, ATT_WIDTH), BF16),
                        pltpu.VMEM((WINDOW + tm, 5 * KV_WIDTH), BF16)],
        compiler_params=pltpu.CompilerParams(
            dimension_semantics=("arbitrary",), vmem_limit_bytes=VMEM_LIMIT),
        name="proj_attn",
    )(sinks, xf, g.reshape(1, d), w, mu.reshape(1, shift_width))


def _split(x, n):
    parts, rem = [], x
    for i in range(n):
        p = rem.astype(CHUNK_DT)
        parts.append(p)
        if i + 1 < n:
            rem = rem - p.astype(F32)
    return parts


_DIMS = {"nn": (((1,), (0,)), ((), ())), "nt": (((1,), (1,)), ((), ())), "tn": (((0,), (0,)), ((), ()))}


def _pdot(ap, bp, dims="nn"):
    order = max(len(ap), len(bp))
    acc = None
    for i, a in enumerate(ap):
        for j, b in enumerate(bp):
            if i + j < order:
                t = lax.dot_general(a, b, _DIMS[dims], preferred_element_type=F32)
                acc = t if acc is None else acc + t
    return acc


def _gdot(x, y, bd_mask, prec, dims="nn"):
    ybd = jnp.where(bd_mask, jnp.concatenate([y] * GROUP, axis=0), 0.0)
    return _pdot(_split(x, prec), _split(ybd, prec), dims)


def _unit_lower_inverse(neg_as, rowi, colh, bd_mask, prec):
    C = CHUNK
    eye = (rowi == colh).astype(F32)
    ds = [jnp.where(rowi // INV_BASE == colh // INV_BASE, na, 0.0) for na in neg_as]
    ts = [eye + d for d in ds]
    ds = [_gdot(d, d, bd_mask, prec) for d in ds]
    span = 4
    while span < INV_BASE:
        xs = [_gdot(jnp.concatenate([t, d], axis=0), d, bd_mask, prec) for t, d in zip(ts, ds)]
        ts = [t + x[:C] for t, x in zip(ts, xs)]
        ds = [x[C:] for x in xs]
        span *= 2
    ts = [t + _gdot(t, d, bd_mask, prec) for t, d in zip(ts, ds)]
    size = INV_BASE
    while size < C:
        mask = (rowi // (2 * size) == colh // (2 * size)) & (rowi // size != colh // size)
        tos = [_gdot(t, jnp.where(mask, na, 0.0), bd_mask, prec) for t, na in zip(ts, neg_as)]
        ts = [t + _gdot(to, t, bd_mask, prec) for t, to in zip(ts, tos)]
        size *= 2
    return ts


def _rwkv_kernel(has_vmix, *refs):
    if has_vmix:
        (xs_ref, vfirst_ref, hs_ref, w0_ref, w2_ref, a0_ref, a2_ref, g2_ref, kk_ref,
         ka_ref, rk_ref, lng_ref, lnb_ref, v0_ref, v1_ref, v2_ref,
         out_ref, state_ref) = refs
    else:
        (xs_ref, hs_ref, w0_ref, w2_ref, a0_ref, a2_ref, g2_ref, kk_ref,
         ka_ref, rk_ref, lng_ref, lnb_ref, out_ref, vout_ref, state_ref) = refs
    C, W, N = CHUNK, RWKV_WIDTH, HEAD_DIM
    R = xs_ref.shape[0]

    @pl.when(pl.program_id(1) == 0)
    def _():
        state_ref[...] = jnp.zeros_like(state_ref)

    ones_bd = hs_ref[...]
    GW = ones_bd.shape[0]
    NG = W // GW
    bd_mask = (lax.broadcasted_iota(jnp.int32, (GW, GW), 0) // N
               == lax.broadcasted_iota(jnp.int32, (GW, GW), 1) // N)
    rowi = lax.broadcasted_iota(jnp.int32, (C, GW), 0)
    colh = lax.broadcasted_iota(jnp.int32, (C, GW), 1) % N
    strict = rowi > colh
    incl = rowi >= colh

    def head_sum(x, prec):
        n = x.shape[0]
        xs_ = jnp.concatenate([x[:, i * GW:(i + 1) * GW] for i in range(NG)], axis=0)
        s = _pdot(_split(xs_, prec), [ones_bd])
        return jnp.concatenate([s[i * n:(i + 1) * n] for i in range(NG)], axis=1)

    def prepare(b0, nb):
        n = nb * C
        xs = xs_ref[b0:b0 + nb].reshape(n, xs_ref.shape[2])
        r = xs[:, 0:W]
        k = xs[:, W:2 * W]
        v = xs[:, 2 * W:3 * W]
        wd = xs[:, 3 * W:3 * W + DECAY_LORA]
        ad = xs[:, 3 * W + DECAY_LORA:3 * W + DECAY_LORA + AAA_LORA]
        gd = xs[:, 3 * W + DECAY_LORA + AAA_LORA:]
        logd = -DECAY_SCALE * jax.nn.sigmoid(w0_ref[...] + _dot(jnp.tanh(wd), w2_ref[...]))
        a = jax.nn.sigmoid(a0_ref[...] + _dot(ad, a2_ref[...]))
        g = _dot(jax.nn.sigmoid(gd), g2_ref[...])
        kk = k * kk_ref[...]
        kk = kk * lax.rsqrt(jnp.maximum(head_sum(kk * kk, PREC_KNORM), KNORM_FLOOR * KNORM_FLOOR))
        k = k * (1.0 + (a - 1.0) * ka_ref[...])
        if has_vmix:
            gate = jax.nn.sigmoid(v0_ref[...] + _dot(_dot(v, v1_ref[...]), v2_ref[...]))
            v = v + (vfirst_ref[b0:b0 + nb].reshape(n, W) - v) * gate
        else:
            vout_ref[b0:b0 + nb] = v.reshape(nb, C, W)
        brow = lax.broadcasted_iota(jnp.int32, (n, n), 0)
        bcol = lax.broadcasted_iota(jnp.int32, (n, n), 1)
        tri = ((brow >= bcol) & (brow // C == bcol // C)).astype(CHUNK_DT)
        cum = _pdot([tri], _split(logd, PREC_CUM))
        grow = jnp.exp(-cum)
        kka = kk * a
        return dict(
            decay_end=[jnp.exp(cum[(b + 1) * C - 1:(b + 1) * C, :]) for b in range(nb)],
            r_t=r * jnp.exp(cum), a_t=-kk * jnp.exp(cum - logd), k_t=k * grow, b_t=kka * grow,
            v=v, g=g, rk=r * k * rk_ref[...])

    def recur(p, b0, nb, after_inverse=None):
        slabs = [(b, i) for b in range(nb) for i in range(NG)]

        def cut(x, b, i):
            return x[b * C:(b + 1) * C, i * GW:(i + 1) * GW]

        ar = [jnp.concatenate([cut(p["a_t"], b, i), cut(p["r_t"], b, i)], axis=0) for b, i in slabs]
        am_b = [_gdot(x, cut(p["b_t"], b, i), bd_mask, PREC_A, "nt") for x, (b, i) in zip(ar, slabs)]
        am_k = [_gdot(x, cut(p["k_t"], b, i), bd_mask, PREC_A, "nt") for x, (b, i) in zip(ar, slabs)]
        tinvs = _unit_lower_inverse([jnp.where(strict, m[:C], 0.0) for m in am_b], rowi, colh, bd_mask, PREC_INV)
        if after_inverse is not None:
            after_inverse()
        a_ak_rk =[jnp.concatenate([jnp.where(strict, m[:C], 0.0), jnp.where(incl, m[C:], 0.0)], axis=0)
                   for m in am_k]
        a_rb = [jnp.where(incl, m[C:], 0.0) for m in am_b]
        sts = [state_ref[b0 + b, i] for b, i in slabs]
        carry = [_gdot(x, st, bd_mask, PREC_APPLY, "nt") for x, st in zip(ar, sts)]
        intra = [_gdot(x, cut(p["v"], b, i), bd_mask, PREC_APPLY) for x, (b, i) in zip(a_ak_rk, slabs)]
        us = [_gdot(ti, cr[:C] + it[:C], bd_mask, PREC_APPLY)
              for ti, cr, it in zip(tinvs, carry, intra)]
        ys = [cr[C:] + it[C:] + _gdot(rb, u, bd_mask, PREC_APPLY)
              for cr, it, rb, u in zip(carry, intra, a_rb, us)]
        for (b, i), st, u in zip(slabs, sts, us):
            uv = jnp.concatenate([u, cut(p["v"], b, i)], axis=0)
            dend = p["decay_end"][b][:, i * GW:(i + 1) * GW]
            bk = jnp.concatenate([cut(p["b_t"], b, i), cut(p["k_t"], b, i)], axis=0) * dend
            full = _pdot(_split(uv, PREC_APPLY), _split(bk, PREC_APPLY), "tn")
            full = jnp.where(bd_mask, full, 0.0)
            st = dend * st
            for h in range(GROUP):
                st = st + full[h * N:(h + 1) * N, :]
            state_ref[b0 + b, i] = st
        return jnp.concatenate([jnp.concatenate(ys[b * NG:(b + 1) * NG], axis=1) for b in range(nb)], axis=0)

    def finish(p, y, b0, nb):
        mean = head_sum(y, PREC_SUM) * (1.0 / N)
        yc = y - mean
        var = head_sum(yc * yc, PREC_SUM) * (1.0 / N)
        y = yc * lax.rsqrt(var + LNX_EPS) * lng_ref[...] + lnb_ref[...]
        bonus = head_sum(p["rk"], PREC_SUM) * p["v"]
        out_ref[b0:b0 + nb] = ((y + bonus) * p["g"]).reshape(nb, C, W)

    parts = [(0, R // 2), (R // 2, R - R // 2)] if R >= 2 * RWKV_ROWS_PER_PART else [(0, R)]
    prepped = [prepare(*parts[0])]
    ys = []
    for n, (b0, nb) in enumerate(parts):
        nxt = (lambda m=n + 1: prepped.append(prepare(*parts[m]))) if n + 1 < len(parts) else None
        ys.append(recur(prepped[n], b0, nb, nxt))
    for p, y, (b0, nb) in zip(prepped, ys, parts):
        finish(p, y, b0, nb)


def _rwkv(rest, vfirst, p, batch, seq):
    n, shift_width = rest.shape
    nc = seq // CHUNK
    W = RWKV_WIDTH
    rows = RWKV_ROWS if batch % RWKV_ROWS == 0 else 1
    tok = lambda b, c: (b, c, 0)
    const = lambda b, c: (0, 0)
    vec = lambda a: a.reshape(1, -1)
    has_vmix = vfirst is not None
    full = lambda a: pl.BlockSpec(a.shape, const)
    gw = GROUP * HEAD_DIM
    hidx = jnp.arange(gw) // HEAD_DIM
    bd_mask = (hidx[:, None] == hidx[None, :]).astype(BF16)

    args = [rest.reshape(batch, seq, shift_width)]
    specs = [pl.BlockSpec((rows, CHUNK, shift_width), tok)]
    if has_vmix:
        args.append(vfirst.reshape(batch, seq, W))
        specs.append(pl.BlockSpec((rows, CHUNK, W), tok))
    params = [bd_mask, vec(p["w0"]), p["w2"], vec(p["a0"]), p["a2"], p["g2"],
              vec(p["k_k"]), vec(p["k_a"]), vec(p["r_k"]), vec(p["lnx_g"]), vec(p["lnx_b"])]
    if has_vmix:
        params += [vec(p["v0"]), p["v1"], p["v2"]]
    args += params
    specs += [full(a) for a in params]

    out_spec = pl.BlockSpec((rows, CHUNK, W), tok)
    out_sds = jax.ShapeDtypeStruct((batch, seq, W), F32)
    outs = pl.pallas_call(
        functools.partial(_rwkv_kernel, has_vmix),
        grid=(batch // rows, nc),
        in_specs=specs,
        out_specs=out_spec if has_vmix else [out_spec, out_spec],
        out_shape=out_sds if has_vmix else [out_sds, out_sds],
        scratch_shapes=[pltpu.VMEM((rows, W // gw, HEAD_DIM, gw), F32)],
        compiler_params=pltpu.CompilerParams(
            dimension_semantics=("arbitrary", "arbitrary"), vmem_limit_bytes=VMEM_LIMIT),
        name="rwkv_vmix" if has_vmix else "rwkv",
    )(*args)
    if has_vmix:
        return outs.reshape(n, W), vfirst
    return outs[0].reshape(n, W), outs[1].reshape(n, W)


def _ffn_kernel(final, *refs):
    if final:
        (x_ref, att_ref, rw_ref, wo_ref, n2g_ref, wg_ref, wu_ref, cw_ref, cb_ref, wd_ref, fg_ref,
         out_ref, halo_ref) = refs
    else:
        (x_ref, att_ref, rw_ref, wo_ref, n2g_ref, wg_ref, wu_ref, cw_ref, cb_ref, wd_ref,
         out_ref, halo_ref) = refs
    tm = x_ref.shape[0]

    @pl.when(pl.program_id(1) == 0)
    def _():
        halo_ref[...] = jnp.zeros_like(halo_ref)

    x = (x_ref[...] + _dot(att_ref[...], wo_ref[:ATT_WIDTH, :])
         + _dot(rw_ref[...], wo_ref[ATT_WIDTH:, :]))
    h = _rms_norm(x, n2g_ref[...]).astype(BF16)
    u = jnp.dot(h, wg_ref[...], preferred_element_type=F32)
    up = jnp.dot(h, wu_ref[...], preferred_element_type=F32)
    ext = jnp.concatenate([halo_ref[...], u], axis=0)
    halo_ref[...] = u[tm - SUBLANES:, :]
    c = (cb_ref[...] + cw_ref[0:1, :] * ext[SUBLANES - 2:SUBLANES - 2 + tm, :]
         + cw_ref[1:2, :] * ext[SUBLANES - 1:SUBLANES - 1 + tm, :] + cw_ref[2:3, :] * u)
    act = (c * jax.nn.sigmoid(c)) * up
    x = x + _dot(act, wd_ref[...])
    if final:
        x = _rms_norm(x, fg_ref[...])
    out_ref[...] = x


def _ffn(xf, att, rw, wo, n2g, wg, wu, cw, cb, wd, fg, batch, seq, tm):
    n, d = xf.shape
    ff = wg.shape[1]
    nt = seq // tm
    tok = lambda b, t: (b * nt + t, 0)
    const = lambda b, t: (0, 0)
    resident = lambda shape: pl.BlockSpec(shape, const, pipeline_mode=pl.Buffered(1))
    final = fg is not None
    args = [xf, att, rw, wo, n2g.reshape(1, d), wg, wu, cw, cb.reshape(1, ff), wd]
    specs = [pl.BlockSpec((tm, d), tok),
             pl.BlockSpec((tm, ATT_WIDTH), tok),
             pl.BlockSpec((tm, RWKV_WIDTH), tok),
             resident(wo.shape), resident((1, d)), resident(wg.shape), resident(wu.shape),
             resident(cw.shape), resident((1, ff)), resident(wd.shape)]
    if final:
        args.append(fg.reshape(1, d))
        specs.append(resident((1, d)))
    return pl.pallas_call(
        functools.partial(_ffn_kernel, final),
        grid=(batch, nt),
        in_specs=specs,
        out_specs=pl.BlockSpec((tm, d), tok),
        out_shape=jax.ShapeDtypeStruct((n, d), F32),
        scratch_shapes=[pltpu.VMEM((SUBLANES, ff), F32)],
        compiler_params=pltpu.CompilerParams(
            dimension_semantics=("arbitrary", "arbitrary"), vmem_limit_bytes=VMEM_LIMIT),
        name="ffn_final" if final else "ffn",
    )(*args)


def kernel(x, norm1_g, w_in, attn_sinks, shift_mu, w0, w2, a0, a2, g2, k_k, k_a, r_k, lnx_g, lnx_b,
           v0, v1, v2, w_out, norm2_g, ffn_w_gate, ffn_w_up, conv_w, conv_b, ffn_w_down, final_g):
    batch, seq, d = x.shape
    depth = w_in.shape[0]
    assert seq % WINDOW == 0 and seq % CHUNK == 0
    tm_proj = tm_ffn = min(ROW_TILE, seq)
    xf = x.reshape(batch * seq, d)
    vfirst = None
    for l in range(depth):
        att, rest = _proj_attn(xf, norm1_g[l], w_in[l].astype(BF16), shift_mu[l], attn_sinks[l], seq, tm_proj)
        p = dict(w0=w0[l], w2=w2[l], a0=a0[l], a2=a2[l], g2=g2[l],
                 k_k=k_k[l], k_a=k_a[l], r_k=r_k[l], lnx_g=lnx_g[l], lnx_b=lnx_b[l])
        if l > 0:
            p.update(v0=v0[l - 1], v1=v1[l - 1], v2=v2[l - 1])
        rw, vfirst = _rwkv(rest, vfirst, p, batch, seq)
        xf = _ffn(xf, att, rw, w_out[l].astype(BF16), norm2_g[l], ffn_w_gate[l].astype(BF16),
                  ffn_w_up[l].astype(BF16), conv_w[l], conv_b[l], ffn_w_down[l].astype(BF16),
                  final_g if l == depth - 1 else None, batch, seq, tm_ffn)
    return xf.reshape(batch, seq, d)
```

```python
import functools
import math

import jax
import jax.numpy as jnp
from jax import lax
from jax.experimental import pallas as pl
from jax.experimental.pallas import tpu as pltpu

F32 = jnp.float32
BF16 = jnp.bfloat16

HEAD_DIM = 64
ATT_HEADS = 8
ATT_KV_HEADS = 2
ATT_GROUP = ATT_HEADS // ATT_KV_HEADS
ATT_WIDTH = ATT_HEADS * HEAD_DIM
KV_WIDTH = ATT_KV_HEADS * HEAD_DIM
WINDOW = 128
RWKV_HEADS = 8
RWKV_WIDTH = RWKV_HEADS * HEAD_DIM
DECAY_LORA = 64
AAA_LORA = 64
GATE_LORA = 128
ATT_IN = ATT_WIDTH + 2 * KV_WIDTH
CONV_WIDTH = 3
RMS_EPS = 1e-5
LNX_EPS = 64e-5
DECAY_SCALE = math.exp(-0.5)
NEG_INF = -1e30

CHUNK = 64
INV_BASE = 8
SUBLANES = 8
RWKV_ROWS = 8
RWKV_ROWS_PER_PART = 4
CHUNK_DT = BF16
PREC_A = 1
PREC_INV = 1
PREC_APPLY = 1
PREC_KNORM = 1
PREC_SUM = 1
PREC_CUM = 2
GROUP = 4
KNORM_FLOOR = 1e-12
ROW_TILE = 512
VMEM_LIMIT = 56 * 1024 * 1024


def _dot(a, b):
    return jnp.dot(a.astype(BF16), b.astype(BF16), preferred_element_type=F32)


def _rms_norm(x, g):
    ms = jnp.mean(x * x, axis=-1, keepdims=True)
    return x * lax.rsqrt(ms + RMS_EPS) * g


def _proj_attn_kernel(tiles_per_seq, n_tiles, sink_ref, x_ref, g_ref, w_ref, mu_ref,
                      att_ref, shifted_ref, halo_ref, halo_used_ref, q_scr, kv_scr, q_stage, kv_stage):
    i = pl.program_id(0)
    tm = x_ref.shape[0]
    Wn, dh = WINDOW, HEAD_DIM
    nblk = tm // Wn

    @pl.when(i == 0)
    def _():
        q_stage[...] = jnp.zeros_like(q_stage)
        kv_stage[...] = jnp.zeros_like(kv_stage)
        halo_used_ref[...] = jnp.zeros_like(halo_used_ref)

    @pl.when(i % tiles_per_seq == 0)
    def _():
        halo_ref[...] = jnp.zeros_like(halo_ref)

    seq_start = (i + tiles_per_seq - 1) % tiles_per_seq == 0
    q_scr[...] = q_stage[...]
    kv_scr[...] = kv_stage[...]
    q_prev = q_scr[...]
    kv_prev = kv_scr[...]
    qpos = lax.broadcasted_iota(jnp.int32, (Wn, 2 * Wn), 0)
    kpos = lax.broadcasted_iota(jnp.int32, (Wn, 2 * Wn), 1)
    dist = qpos + Wn - kpos
    in_window = (dist >= 0) & (dist < Wn)
    distf = dist.astype(F32)
    lane = lax.broadcasted_iota(jnp.int32, (Wn, 2 * dh), 1)
    low = lane < dh
    scores = {}
    for b in range(nblk):
        for j in range(ATT_KV_HEADS):
            qs = jnp.concatenate([q_prev[b * Wn:(b + 1) * Wn, h * dh:(h + 1) * dh]
                                  for h in range(j * ATT_GROUP, (j + 1) * ATT_GROUP)], axis=0)
            k = kv_prev[b * Wn:(b + 2) * Wn, j * dh:(j + 1) * dh]
            scores[b, j] = lax.dot_general(qs, k, (((1,), (1,)), ((), ())), preferred_element_type=F32)

    hn = _rms_norm(x_ref[...], g_ref[...]).astype(BF16)
    q_new = jnp.dot(hn, w_ref[:, :ATT_WIDTH], preferred_element_type=F32)
    kv_new = jnp.dot(hn, w_ref[:, ATT_WIDTH:ATT_IN], preferred_element_type=F32)
    rest = jnp.dot(hn, w_ref[:, ATT_IN:], preferred_element_type=F32)
    halo = jnp.where(i == n_tiles, halo_used_ref[...], halo_ref[...])
    halo_used_ref[...] = halo
    trow = lax.broadcasted_iota(jnp.int32, rest.shape, 0)
    prev = jnp.where(trow == 0, halo[SUBLANES - 1:SUBLANES, :], pltpu.roll(rest, 1, 0))
    halo_ref[...] = rest[tm - SUBLANES:, :]
    shifted_ref[...] = rest + (prev - rest) * mu_ref[...]

    probs, sink_terms = {}, {}
    for b in range(nblk):
        valid = in_window & ((kpos >= Wn) | jnp.logical_not(seq_start)) if b == 0 else in_window
        for j in range(ATT_KV_HEADS):
            for g in range(ATT_GROUP):
                h = j * ATT_GROUP + g
                s = scores[b, j][g * Wn:(g + 1) * Wn, :] - (2.0 ** (-8.0 * (h + 1) / ATT_HEADS)) * distf
                s = jnp.where(valid, s, NEG_INF)
                sink = sink_ref[h]
                m = jnp.maximum(jnp.max(s, axis=-1, keepdims=True), sink)
                probs[b, h] = jnp.exp(s - m).astype(BF16)
                sink_terms[b, h] = jnp.exp(sink - m)
    v_even = [kv_prev[:, KV_WIDTH + 4 * j * dh:KV_WIDTH + (4 * j + 2) * dh] for j in range(ATT_KV_HEADS)]
    v_odd = [kv_prev[:, KV_WIDTH + (4 * j + 2) * dh:KV_WIDTH + (4 * j + 4) * dh] for j in range(ATT_KV_HEADS)]
    rows = []
    for b in range(nblk):
        cols = []
        for j in range(ATT_KV_HEADS):
            hs = range(j * ATT_GROUP, (j + 1) * ATT_GROUP)
            pe = jnp.concatenate([probs[b, h] for h in hs[0::2]], axis=0)
            po = jnp.concatenate([probs[b, h] for h in hs[1::2]], axis=0)
            oe = jnp.dot(pe, v_even[j][b * Wn:(b + 2) * Wn], preferred_element_type=F32)
            oo = jnp.dot(po, v_odd[j][b * Wn:(b + 2) * Wn], preferred_element_type=F32)
            for pair in range(ATT_GROUP // 2):
                e = oe[pair * Wn:(pair + 1) * Wn, :]
                o = oo[pair * Wn:(pair + 1) * Wn, :]
                num = jnp.where(low, e, o)
                den = pltpu.roll(jnp.where(low, o, e), dh, 1)
                den = den + jnp.where(low, sink_terms[b, hs[2 * pair]], sink_terms[b, hs[2 * pair + 1]])
                cols.append(num / den)
        rows.append(jnp.concatenate(cols, axis=1))
    att_ref[...] = jnp.concatenate(rows, axis=0)

    ones = jnp.ones((tm, dh), F32)
    pieces = [kv_new[:, :KV_WIDTH]]
    for j in range(ATT_KV_HEADS):
        v = kv_new[:, KV_WIDTH + j * dh:KV_WIDTH + (j + 1) * dh]
        pieces += [jnp.concatenate([v, ones], axis=1), jnp.concatenate([ones, v], axis=1)]
    kv_stage[0:Wn, :] = kv_prev[tm:tm + Wn, :]
    kv_stage[Wn:, :] = jnp.concatenate(pieces, axis=1).astype(BF16)
    q_stage[...] = (q_new * (dh ** -0.5)).astype(BF16)


def _proj_attn(xf, g, w, layer, mu, sinks, seq, tm):
    n, d = xf.shape
    in_width = w.shape[2]
    shift_width = in_width - ATT_IN
    n_tiles = n // tm
    const = lambda i: (0, 0)
    cur = lambda i: (jnp.minimum(i, n_tiles - 1), 0)
    lag = lambda i: (jnp.maximum(i - 1, 0), 0)
    return pl.pallas_call(
        functools.partial(_proj_attn_kernel, seq // tm, n_tiles),
        grid=(n_tiles + 1,),
        in_specs=[pl.BlockSpec(memory_space=pltpu.SMEM),
                  pl.BlockSpec((tm, d), cur),
                  pl.BlockSpec((1, d), const),
                  pl.BlockSpec((None, d, in_width), lambda i: (layer, 0, 0)),
                  pl.BlockSpec((1, shift_width), const)],
        out_specs=[pl.BlockSpec((tm, ATT_WIDTH), lag),
                   pl.BlockSpec((tm, shift_width), cur)],
        out_shape=[jax.ShapeDtypeStruct((n, ATT_WIDTH), F32),
                   jax.ShapeDtypeStruct((n, shift_width), F32)],
        scratch_shapes=[pltpu.VMEM((SUBLANES, shift_width), F32),
                        pltpu.VMEM((SUBLANES, shift_width), F32),
                        pltpu.VMEM((tm, ATT_WIDTH), BF16),
                        pltpu.VMEM((WINDOW + tm, 5 * KV_WIDTH), BF16),
                        pltpu.VMEM((tm, ATT_WIDTH), BF16),
                        pltpu.VMEM((WINDOW + tm, 5 * KV_WIDTH), BF16)],
        compiler_params=pltpu.CompilerParams(
            dimension_semantics=("arbitrary",), vmem_limit_bytes=VMEM_LIMIT),
        name="proj_attn",
    )(sinks, xf, g.reshape(1, d), w, mu.reshape(1, shift_width))


def _split(x, n):
    parts, rem = [], x
    for i in range(n):
        p = rem.astype(CHUNK_DT)
        parts.append(p)
        if i + 1 < n:
            rem = rem - p.astype(F32)
    return parts


_DIMS = {"nn": (((1,), (0,)), ((), ())), "nt": (((1,), (1,)), ((), ())), "tn": (((0,), (0,)), ((), ()))}


def _pdot(ap, bp, dims="nn"):
    order = max(len(ap), len(bp))
    acc = None
    for i, a in enumerate(ap):
        for j, b in enumerate(bp):
            if i + j < order:
                t = lax.dot_general(a, b, _DIMS[dims], preferred_element_type=F32)
                acc = t if acc is None else acc + t
    return acc


def _gdot(x, y, bd_mask, prec, dims="nn"):
    ybd = jnp.where(bd_mask, jnp.concatenate([y] * GROUP, axis=0), 0.0)
    return _pdot(_split(x, prec), _split(ybd, prec), dims)


def _unit_lower_inverse(neg_as, rowi, colh, bd_mask, prec):
    C = CHUNK
    eye = (rowi == colh).astype(F32)
    ds = [jnp.where(rowi // INV_BASE == colh // INV_BASE, na, 0.0) for na in neg_as]
    ts = [eye + d for d in ds]
    ds = [_gdot(d, d, bd_mask, prec) for d in ds]
    span = 4
    while span < INV_BASE:
        xs = [_gdot(jnp.concatenate([t, d], axis=0), d, bd_mask, prec) for t, d in zip(ts, ds)]
        ts = [t + x[:C] for t, x in zip(ts, xs)]
        ds = [x[C:] for x in xs]
        span *= 2
    ts = [t + _gdot(t, d, bd_mask, prec) for t, d in zip(ts, ds)]
    size = INV_BASE
    while size < C:
        mask = (rowi // (2 * size) == colh // (2 * size)) & (rowi // size != colh // size)
        tos = [_gdot(t, jnp.where(mask, na, 0.0), bd_mask, prec) for t, na in zip(ts, neg_as)]
        ts = [t + _gdot(to, t, bd_mask, prec) for t, to in zip(ts, tos)]
        size *= 2
    return ts


def _rwkv_kernel(has_vmix, *refs):
    if has_vmix:
        (xs_ref, vfirst_ref, hs_ref, w0_ref, w2_ref, a0_ref, a2_ref, g2_ref, kk_ref,
         ka_ref, rk_ref, lng_ref, lnb_ref, v0_ref, v1_ref, v2_ref,
         out_ref, state_ref) = refs
    else:
        (xs_ref, hs_ref, w0_ref, w2_ref, a0_ref, a2_ref, g2_ref, kk_ref,
         ka_ref, rk_ref, lng_ref, lnb_ref, out_ref, vout_ref, state_ref) = refs
    C, W, N = CHUNK, RWKV_WIDTH, HEAD_DIM
    R = xs_ref.shape[0]

    @pl.when(pl.program_id(1) == 0)
    def _():
        state_ref[...] = jnp.zeros_like(state_ref)

    ones_bd = hs_ref[...]
    GW = ones_bd.shape[0]
    NG = W // GW
    bd_mask = (lax.broadcasted_iota(jnp.int32, (GW, GW), 0) // N
               == lax.broadcasted_iota(jnp.int32, (GW, GW), 1) // N)
    rowi = lax.broadcasted_iota(jnp.int32, (C, GW), 0)
    colh = lax.broadcasted_iota(jnp.int32, (C, GW), 1) % N
    strict = rowi > colh
    incl = rowi >= colh

    def head_sum(x, prec):
        n = x.shape[0]
        xs_ = jnp.concatenate([x[:, i * GW:(i + 1) * GW] for i in range(NG)], axis=0)
        s = _pdot(_split(xs_, prec), [ones_bd])
        return jnp.concatenate([s[i * n:(i + 1) * n] for i in range(NG)], axis=1)

    def prepare(b0, nb):
        n = nb * C
        xs = xs_ref[b0:b0 + nb].reshape(n, xs_ref.shape[2])
        r = xs[:, 0:W]
        k = xs[:, W:2 * W]
        v = xs[:, 2 * W:3 * W]
        wd = xs[:, 3 * W:3 * W + DECAY_LORA]
        ad = xs[:, 3 * W + DECAY_LORA:3 * W + DECAY_LORA + AAA_LORA]
        gd = xs[:, 3 * W + DECAY_LORA + AAA_LORA:]
        logd = -DECAY_SCALE * jax.nn.sigmoid(w0_ref[...] + _dot(jnp.tanh(wd), w2_ref[...]))
        a = jax.nn.sigmoid(a0_ref[...] + _dot(ad, a2_ref[...]))
        g = _dot(jax.nn.sigmoid(gd), g2_ref[...])
        kk = k * kk_ref[...]
        kk = kk * lax.rsqrt(jnp.maximum(head_sum(kk * kk, PREC_KNORM), KNORM_FLOOR * KNORM_FLOOR))
        k = k * (1.0 + (a - 1.0) * ka_ref[...])
        if has_vmix:
            gate = jax.nn.sigmoid(v0_ref[...] + _dot(_dot(v, v1_ref[...]), v2_ref[...]))
            v = v + (vfirst_ref[b0:b0 + nb].reshape(n, W) - v) * gate
        else:
            vout_ref[b0:b0 + nb] = v.reshape(nb, C, W)
        brow = lax.broadcasted_iota(jnp.int32, (n, n), 0)
        bcol = lax.broadcasted_iota(jnp.int32, (n, n), 1)
        tri = ((brow >= bcol) & (brow // C == bcol // C)).astype(CHUNK_DT)
        cum = _pdot([tri], _split(logd, PREC_CUM))
        grow = jnp.exp(-cum)
        kka = kk * a
        return dict(
            decay_end=[jnp.exp(cum[(b + 1) * C - 1:(b + 1) * C, :]) for b in range(nb)],
            r_t=r * jnp.exp(cum), a_t=-kk * jnp.exp(cum - logd), k_t=k * grow, b_t=kka * grow,
            v=v, g=g, rk=r * k * rk_ref[...])

    def recur(p, b0, nb, after_inverse=None):
        slabs = [(b, i) for b in range(nb) for i in range(NG)]

        def cut(x, b, i):
            return x[b * C:(b + 1) * C, i * GW:(i + 1) * GW]

        ar = [jnp.concatenate([cut(p["a_t"], b, i), cut(p["r_t"], b, i)], axis=0) for b, i in slabs]
        am_b = [_gdot(x, cut(p["b_t"], b, i), bd_mask, PREC_A, "nt") for x, (b, i) in zip(ar, slabs)]
        am_k = [_gdot(x, cut(p["k_t"], b, i), bd_mask, PREC_A, "nt") for x, (b, i) in zip(ar, slabs)]
        tinvs = _unit_lower_inverse([jnp.where(strict, m[:C], 0.0) for m in am_b], rowi, colh, bd_mask, PREC_INV)
        if after_inverse is not None:
            after_inverse()
        a_ak_rk =[jnp.concatenate([jnp.where(strict, m[:C], 0.0), jnp.where(incl, m[C:], 0.0)], axis=0)
                   for m in am_k]
        a_rb = [jnp.where(incl, m[C:], 0.0) for m in am_b]
        sts = [state_ref[b0 + b, i] for b, i in slabs]
        carry = [_gdot(x, st, bd_mask, PREC_APPLY, "nt") for x, st in zip(ar, sts)]
        intra = [_gdot(x, cut(p["v"], b, i), bd_mask, PREC_APPLY) for x, (b, i) in zip(a_ak_rk, slabs)]
        us = [_gdot(ti, cr[:C] + it[:C], bd_mask, PREC_APPLY)
              for ti, cr, it in zip(tinvs, carry, intra)]
        ys = [cr[C:] + it[C:] + _gdot(rb, u, bd_mask, PREC_APPLY)
              for cr, it, rb, u in zip(carry, intra, a_rb, us)]
        for (b, i), st, u in zip(slabs, sts, us):
            uv = jnp.concatenate([u, cut(p["v"], b, i)], axis=0)
            dend = p["decay_end"][b][:, i * GW:(i + 1) * GW]
            bk = jnp.concatenate([cut(p["b_t"], b, i), cut(p["k_t"], b, i)], axis=0) * dend
            full = _pdot(_split(uv, PREC_APPLY), _split(bk, PREC_APPLY), "tn")
            full = jnp.where(bd_mask, full, 0.0)
            st = dend * st
            for h in range(GROUP):
                st = st + full[h * N:(h + 1) * N, :]
            state_ref[b0 + b, i] = st
        return jnp.concatenate([jnp.concatenate(ys[b * NG:(b + 1) * NG], axis=1) for b in range(nb)], axis=0)

    def finish(p, y, b0, nb):
        mean = head_sum(y, PREC_SUM) * (1.0 / N)
        yc = y - mean
        var = head_sum(yc * yc, PREC_SUM) * (1.0 / N)
        y = yc * lax.rsqrt(var + LNX_EPS) * lng_ref[...] + lnb_ref[...]
        bonus = head_sum(p["rk"], PREC_SUM) * p["v"]
        out_ref[b0:b0 + nb] = ((y + bonus) * p["g"]).reshape(nb, C, W)

    parts = [(0, R // 2), (R // 2, R - R // 2)] if R >= 2 * RWKV_ROWS_PER_PART else [(0, R)]
    prepped = [prepare(*parts[0])]
    ys = []
    for n, (b0, nb) in enumerate(parts):
        nxt = (lambda m=n + 1: prepped.append(prepare(*parts[m]))) if n + 1 < len(parts) else None
        ys.append(recur(prepped[n], b0, nb, nxt))
    for p, y, (b0, nb) in zip(prepped, ys, parts):
        finish(p, y, b0, nb)


def _rwkv(rest, vfirst, p, batch, seq):
    n, shift_width = rest.shape
    nc = seq // CHUNK
    W = RWKV_WIDTH
    rows = RWKV_ROWS if batch % RWKV_ROWS == 0 else 1
    tok = lambda b, c: (b, c, 0)
    const = lambda b, c: (0, 0)
    vec = lambda a: a.reshape(1, -1)
    has_vmix = vfirst is not None
    full = lambda a: pl.BlockSpec(a.shape, const)
    gw = GROUP * HEAD_DIM
    hidx = jnp.arange(gw) // HEAD_DIM
    bd_mask = (hidx[:, None] == hidx[None, :]).astype(BF16)

    args = [rest.reshape(batch, seq, shift_width)]
    specs = [pl.BlockSpec((rows, CHUNK, shift_width), tok)]
    if has_vmix:
        args.append(vfirst.reshape(batch, seq, W))
        specs.append(pl.BlockSpec((rows, CHUNK, W), tok))
    params = [bd_mask, vec(p["w0"]), p["w2"], vec(p["a0"]), p["a2"], p["g2"],
              vec(p["k_k"]), vec(p["k_a"]), vec(p["r_k"]), vec(p["lnx_g"]), vec(p["lnx_b"])]
    if has_vmix:
        params += [vec(p["v0"]), p["v1"], p["v2"]]
    args += params
    specs += [full(a) for a in params]

    out_spec = pl.BlockSpec((rows, CHUNK, W), tok)
    out_sds = jax.ShapeDtypeStruct((batch, seq, W), F32)
    outs = pl.pallas_call(
        functools.partial(_rwkv_kernel, has_vmix),
        grid=(batch // rows, nc),
        in_specs=specs,
        out_specs=out_spec if has_vmix else [out_spec, out_spec],
        out_shape=out_sds if has_vmix else [out_sds, out_sds],
        scratch_shapes=[pltpu.VMEM((rows, W // gw, HEAD_DIM, gw), F32)],
        compiler_params=pltpu.CompilerParams(
            dimension_semantics=("arbitrary", "arbitrary"), vmem_limit_bytes=VMEM_LIMIT),
        name="rwkv_vmix" if has_vmix else "rwkv",
    )(*args)
    if has_vmix:
        return outs.reshape(n, W), vfirst
    return outs[0].reshape(n, W), outs[1].reshape(n, W)


def _ffn_kernel(final, *refs):
    if final:
        (x_ref, att_ref, rw_ref, wo_ref, n2g_ref, wg_ref, wu_ref, cw_ref, cb_ref, wd_ref, fg_ref,
         out_ref, halo_ref) = refs
    else:
        (x_ref, att_ref, rw_ref, wo_ref, n2g_ref, wg_ref, wu_ref, cw_ref, cb_ref, wd_ref,
         out_ref, halo_ref) = refs
    tm = x_ref.shape[0]

    @pl.when(pl.program_id(1) == 0)
    def _():
        halo_ref[...] = jnp.zeros_like(halo_ref)

    x = (x_ref[...] + _dot(att_ref[...], wo_ref[:ATT_WIDTH, :])
         + _dot(rw_ref[...], wo_ref[ATT_WIDTH:, :]))
    h = _rms_norm(x, n2g_ref[...]).astype(BF16)
    u = jnp.dot(h, wg_ref[...], preferred_element_type=F32)
    up = jnp.dot(h, wu_ref[...], preferred_element_type=F32)
    ext = jnp.concatenate([halo_ref[...], u], axis=0)
    halo_ref[...] = u[tm - SUBLANES:, :]
    c = (cb_ref[...] + cw_ref[0:1, :] * ext[SUBLANES - 2:SUBLANES - 2 + tm, :]
         + cw_ref[1:2, :] * ext[SUBLANES - 1:SUBLANES - 1 + tm, :] + cw_ref[2:3, :] * u)
    act = (c * jax.nn.sigmoid(c)) * up
    x = x + _dot(act, wd_ref[...])
    if final:
        x = _rms_norm(x, fg_ref[...])
    out_ref[...] = x


def _ffn(xf, att, rw, layer, wo, n2g, wg, wu, cw, cb, wd, fg, batch, seq, tm):
    n, d = xf.shape
    ff = wg.shape[2]
    nt = seq // tm
    tok = lambda b, t: (b * nt + t, 0)
    const = lambda b, t: (0, 0)
    resident = lambda shape: pl.BlockSpec(shape, const, pipeline_mode=pl.Buffered(1))
    stacked = lambda w: pl.BlockSpec((None,) + w.shape[1:], lambda b, t: (layer, 0, 0),
                                     pipeline_mode=pl.Buffered(1))
    final = fg is not None
    args = [xf, att, rw, wo, n2g.reshape(1, d), wg, wu, cw, cb.reshape(1, ff), wd]
    specs = [pl.BlockSpec((tm, d), tok),
             pl.BlockSpec((tm, ATT_WIDTH), tok),
             pl.BlockSpec((tm, RWKV_WIDTH), tok),
             stacked(wo), resident((1, d)), stacked(wg), stacked(wu),
             resident(cw.shape), resident((1, ff)), stacked(wd)]
    if final:
        args.append(fg.reshape(1, d))
        specs.append(resident((1, d)))
    return pl.pallas_call(
        functools.partial(_ffn_kernel, final),
        grid=(batch, nt),
        in_specs=specs,
        out_specs=pl.BlockSpec((tm, d), tok),
        out_shape=jax.ShapeDtypeStruct((n, d), F32),
        scratch_shapes=[pltpu.VMEM((SUBLANES, ff), F32)],
        compiler_params=pltpu.CompilerParams(
            dimension_semantics=("arbitrary", "arbitrary"), vmem_limit_bytes=VMEM_LIMIT),
        name="ffn_final" if final else "ffn",
    )(*args)


def kernel(x, norm1_g, w_in, attn_sinks, shift_mu, w0, w2, a0, a2, g2, k_k, k_a, r_k, lnx_g, lnx_b,
           v0, v1, v2, w_out, norm2_g, ffn_w_gate, ffn_w_up, conv_w, conv_b, ffn_w_down, final_g):
    batch, seq, d = x.shape
    depth = w_in.shape[0]
    assert seq % WINDOW == 0 and seq % CHUNK == 0
    tm_proj = tm_ffn = min(ROW_TILE, seq)
    xf = x.reshape(batch * seq, d)
    w_in, w_out, ffn_w_gate, ffn_w_up, ffn_w_down = (
        w.astype(BF16) for w in (w_in, w_out, ffn_w_gate, ffn_w_up, ffn_w_down))
    vfirst = None
    for l in range(depth):
        att, rest = _proj_attn(xf, norm1_g[l], w_in, l, shift_mu[l], attn_sinks[l], seq, tm_proj)
        p = dict(w0=w0[l], w2=w2[l], a0=a0[l], a2=a2[l], g2=g2[l],
                 k_k=k_k[l], k_a=k_a[l], r_k=r_k[l], lnx_g=lnx_g[l], lnx_b=lnx_b[l])
        if l > 0:
            p.update(v0=v0[l - 1], v1=v1[l - 1], v2=v2[l - 1])
        rw, vfirst = _rwkv(rest, vfirst, p, batch, seq)
        xf = _ffn(xf, att, rw, l, w_out, norm2_g[l], ffn_w_gate, ffn_w_up, conv_w[l], conv_b[l], ffn_w_down,
                  final_g if l == depth - 1 else None, batch, seq, tm_ffn)
    return xf.reshape(batch, seq, d)
```

```python
import functools
import math

import jax
import jax.numpy as jnp
from jax import lax
from jax.experimental import pallas as pl
from jax.experimental.pallas import tpu as pltpu

F32 = jnp.float32
BF16 = jnp.bfloat16

HEAD_DIM = 64
ATT_HEADS = 8
ATT_KV_HEADS = 2
ATT_GROUP = ATT_HEADS // ATT_KV_HEADS
ATT_WIDTH = ATT_HEADS * HEAD_DIM
KV_WIDTH = ATT_KV_HEADS * HEAD_DIM
WINDOW = 128
RWKV_HEADS = 8
RWKV_WIDTH = RWKV_HEADS * HEAD_DIM
DECAY_LORA = 64
AAA_LORA = 64
GATE_LORA = 128
ATT_IN = ATT_WIDTH + 2 * KV_WIDTH
CONV_WIDTH = 3
RMS_EPS = 1e-5
LNX_EPS = 64e-5
DECAY_SCALE = math.exp(-0.5)
NEG_INF = -1e30

CHUNK = 64
INV_BASE = 8
SUBLANES = 8
RWKV_ROWS = 8
RWKV_INPUT_BUFFERS = 3
RWKV_ROWS_PER_PART = 4
CHUNK_DT = BF16
PREC_A = 1
PREC_INV = 1
PREC_APPLY = 1
PREC_KNORM = 1
PREC_SUM = 1
PREC_CUM = 2
GROUP = 4
KNORM_FLOOR = 1e-12
ROW_TILE = 512
VMEM_LIMIT = 56 * 1024 * 1024


def _dot(a, b):
    return jnp.dot(a.astype(BF16), b.astype(BF16), preferred_element_type=F32)


def _rms_norm(x, g):
    ms = jnp.mean(x * x, axis=-1, keepdims=True)
    return x * lax.rsqrt(ms + RMS_EPS) * g


def _proj_attn_kernel(tiles_per_seq, n_tiles, sink_ref, x_ref, g_ref, w_ref, mu_ref,
                      att_ref, shifted_ref, halo_ref, halo_used_ref, q_scr, kv_scr, q_stage, kv_stage):
    i = pl.program_id(0)
    tm = x_ref.shape[0]
    Wn, dh = WINDOW, HEAD_DIM
    nblk = tm // Wn

    @pl.when(i == 0)
    def _():
        q_stage[...] = jnp.zeros_like(q_stage)
        kv_stage[...] = jnp.zeros_like(kv_stage)
        halo_used_ref[...] = jnp.zeros_like(halo_used_ref)

    @pl.when(i % tiles_per_seq == 0)
    def _():
        halo_ref[...] = jnp.zeros_like(halo_ref)

    seq_start = (i + tiles_per_seq - 1) % tiles_per_seq == 0
    q_scr[...] = q_stage[...]
    kv_scr[...] = kv_stage[...]
    q_prev = q_scr[...]
    kv_prev = kv_scr[...]
    qpos = lax.broadcasted_iota(jnp.int32, (Wn, 2 * Wn), 0)
    kpos = lax.broadcasted_iota(jnp.int32, (Wn, 2 * Wn), 1)
    dist = qpos + Wn - kpos
    in_window = (dist >= 0) & (dist < Wn)
    distf = dist.astype(F32)
    lane = lax.broadcasted_iota(jnp.int32, (Wn, 2 * dh), 1)
    low = lane < dh
    scores = {}
    for b in range(nblk):
        for j in range(ATT_KV_HEADS):
            qs = jnp.concatenate([q_prev[b * Wn:(b + 1) * Wn, h * dh:(h + 1) * dh]
                                  for h in range(j * ATT_GROUP, (j + 1) * ATT_GROUP)], axis=0)
            k = kv_prev[b * Wn:(b + 2) * Wn, j * dh:(j + 1) * dh]
            scores[b, j] = lax.dot_general(qs, k, (((1,), (1,)), ((), ())), preferred_element_type=F32)

    hn = _rms_norm(x_ref[...], g_ref[...]).astype(BF16)
    q_new = jnp.dot(hn, w_ref[:, :ATT_WIDTH], preferred_element_type=F32)
    kv_new = jnp.dot(hn, w_ref[:, ATT_WIDTH:ATT_IN], preferred_element_type=F32)
    rest = jnp.dot(hn, w_ref[:, ATT_IN:], preferred_element_type=F32)
    halo = jnp.where(i == n_tiles, halo_used_ref[...], halo_ref[...])
    halo_used_ref[...] = halo
    trow = lax.broadcasted_iota(jnp.int32, rest.shape, 0)
    prev = jnp.where(trow == 0, halo[SUBLANES - 1:SUBLANES, :], pltpu.roll(rest, 1, 0))
    halo_ref[...] = rest[tm - SUBLANES:, :]
    shifted_ref[...] = rest + (prev - rest) * mu_ref[...]

    probs, sink_terms = {}, {}
    for b in range(nblk):
        valid = in_window & ((kpos >= Wn) | jnp.logical_not(seq_start)) if b == 0 else in_window
        for j in range(ATT_KV_HEADS):
            for g in range(ATT_GROUP):
                h = j * ATT_GROUP + g
                s = scores[b, j][g * Wn:(g + 1) * Wn, :] - (2.0 ** (-8.0 * (h + 1) / ATT_HEADS)) * distf
                s = jnp.where(valid, s, NEG_INF)
                sink = sink_ref[h]
                m = jnp.maximum(jnp.max(s, axis=-1, keepdims=True), sink)
                probs[b, h] = jnp.exp(s - m).astype(BF16)
                sink_terms[b, h] = jnp.exp(sink - m)
    v_even = [kv_prev[:, KV_WIDTH + 4 * j * dh:KV_WIDTH + (4 * j + 2) * dh] for j in range(ATT_KV_HEADS)]
    v_odd = [kv_prev[:, KV_WIDTH + (4 * j + 2) * dh:KV_WIDTH + (4 * j + 4) * dh] for j in range(ATT_KV_HEADS)]
    rows = []
    for b in range(nblk):
        cols = []
        for j in range(ATT_KV_HEADS):
            hs = range(j * ATT_GROUP, (j + 1) * ATT_GROUP)
            pe = jnp.concatenate([probs[b, h] for h in hs[0::2]], axis=0)
            po = jnp.concatenate([probs[b, h] for h in hs[1::2]], axis=0)
            oe = jnp.dot(pe, v_even[j][b * Wn:(b + 2) * Wn], preferred_element_type=F32)
            oo = jnp.dot(po, v_odd[j][b * Wn:(b + 2) * Wn], preferred_element_type=F32)
            for pair in range(ATT_GROUP // 2):
                e = oe[pair * Wn:(pair + 1) * Wn, :]
                o = oo[pair * Wn:(pair + 1) * Wn, :]
                num = jnp.where(low, e, o)
                den = pltpu.roll(jnp.where(low, o, e), dh, 1)
                den = den + jnp.where(low, sink_terms[b, hs[2 * pair]], sink_terms[b, hs[2 * pair + 1]])
                cols.append(num / den)
        rows.append(jnp.concatenate(cols, axis=1))
    att_ref[...] = jnp.concatenate(rows, axis=0)

    ones = jnp.ones((tm, dh), F32)
    pieces = [kv_new[:, :KV_WIDTH]]
    for j in range(ATT_KV_HEADS):
        v = kv_new[:, KV_WIDTH + j * dh:KV_WIDTH + (j + 1) * dh]
        pieces += [jnp.concatenate([v, ones], axis=1), jnp.concatenate([ones, v], axis=1)]
    kv_stage[0:Wn, :] = kv_prev[tm:tm + Wn, :]
    kv_stage[Wn:, :] = jnp.concatenate(pieces, axis=1).astype(BF16)
    q_stage[...] = (q_new * (dh ** -0.5)).astype(BF16)


def _proj_attn(xf, g, w, layer, mu, sinks, seq, tm):
    n, d = xf.shape
    in_width = w.shape[2]
    shift_width = in_width - ATT_IN
    n_tiles = n // tm
    const = lambda i: (0, 0)
    cur = lambda i: (jnp.minimum(i, n_tiles - 1), 0)
    lag = lambda i: (jnp.maximum(i - 1, 0), 0)
    return pl.pallas_call(
        functools.partial(_proj_attn_kernel, seq // tm, n_tiles),
        grid=(n_tiles + 1,),
        in_specs=[pl.BlockSpec(memory_space=pltpu.SMEM),
                  pl.BlockSpec((tm, d), cur),
                  pl.BlockSpec((1, d), const),
                  pl.BlockSpec((None, d, in_width), lambda i: (layer, 0, 0)),
                  pl.BlockSpec((1, shift_width), const)],
        out_specs=[pl.BlockSpec((tm, ATT_WIDTH), lag),
                   pl.BlockSpec((tm, shift_width), cur)],
        out_shape=[jax.ShapeDtypeStruct((n, ATT_WIDTH), F32),
                   jax.ShapeDtypeStruct((n, shift_width), F32)],
        scratch_shapes=[pltpu.VMEM((SUBLANES, shift_width), F32),
                        pltpu.VMEM((SUBLANES, shift_width), F32),
                        pltpu.VMEM((tm, ATT_WIDTH), BF16),
                        pltpu.VMEM((WINDOW + tm, 5 * KV_WIDTH), BF16),
                        pltpu.VMEM((tm, ATT_WIDTH), BF16),
                        pltpu.VMEM((WINDOW + tm, 5 * KV_WIDTH), BF16)],
        compiler_params=pltpu.CompilerParams(
            dimension_semantics=("arbitrary",), vmem_limit_bytes=VMEM_LIMIT),
        name="proj_attn",
    )(sinks, xf, g.reshape(1, d), w, mu.reshape(1, shift_width))


def _split(x, n):
    parts, rem = [], x
    for i in range(n):
        p = rem.astype(CHUNK_DT)
        parts.append(p)
        if i + 1 < n:
            rem = rem - p.astype(F32)
    return parts


_DIMS = {"nn": (((1,), (0,)), ((), ())), "nt": (((1,), (1,)), ((), ())), "tn": (((0,), (0,)), ((), ()))}


def _pdot(ap, bp, dims="nn"):
    order = max(len(ap), len(bp))
    acc = None
    for i, a in enumerate(ap):
        for j, b in enumerate(bp):
            if i + j < order:
                t = lax.dot_general(a, b, _DIMS[dims], preferred_element_type=F32)
                acc = t if acc is None else acc + t
    return acc


def _gdot(x, y, bd_mask, prec, dims="nn"):
    ybd = jnp.where(bd_mask, jnp.concatenate([y] * GROUP, axis=0), 0.0)
    return _pdot(_split(x, prec), _split(ybd, prec), dims)


def _unit_lower_inverse(neg_as, rowi, colh, bd_mask, prec):
    C = CHUNK
    eye = (rowi == colh).astype(F32)
    ds = [jnp.where(rowi // INV_BASE == colh // INV_BASE, na, 0.0) for na in neg_as]
    ts = [eye + d for d in ds]
    ds = [_gdot(d, d, bd_mask, prec) for d in ds]
    span = 4
    while span < INV_BASE:
        xs = [_gdot(jnp.concatenate([t, d], axis=0), d, bd_mask, prec) for t, d in zip(ts, ds)]
        ts = [t + x[:C] for t, x in zip(ts, xs)]
        ds = [x[C:] for x in xs]
        span *= 2
    ts = [t + _gdot(t, d, bd_mask, prec) for t, d in zip(ts, ds)]
    size = INV_BASE
    while size < C:
        mask = (rowi // (2 * size) == colh // (2 * size)) & (rowi // size != colh // size)
        tos = [_gdot(t, jnp.where(mask, na, 0.0), bd_mask, prec) for t, na in zip(ts, neg_as)]
        ts = [t + _gdot(to, t, bd_mask, prec) for t, to in zip(ts, tos)]
        size *= 2
    return ts


def _rwkv_kernel(has_vmix, *refs):
    if has_vmix:
        (xs_ref, vfirst_ref, hs_ref, w0_ref, w2_ref, a0_ref, a2_ref, g2_ref, kk_ref,
         ka_ref, rk_ref, lng_ref, lnb_ref, v0_ref, v1_ref, v2_ref,
         out_ref, state_ref, xs_buf, xs_sem) = refs
    else:
        (xs_ref, hs_ref, w0_ref, w2_ref, a0_ref, a2_ref, g2_ref, kk_ref,
         ka_ref, rk_ref, lng_ref, lnb_ref, out_ref, vout_ref, state_ref, xs_buf, xs_sem) = refs
    C, W, N = CHUNK, RWKV_WIDTH, HEAD_DIM
    nbuf, R = xs_buf.shape[0], xs_buf.shape[1]

    nc = pl.num_programs(1)
    total = pl.num_programs(0) * nc
    step = pl.program_id(0) * nc + pl.program_id(1)

    def block_copy(s):
        slot = s % nbuf
        return pltpu.make_async_copy(
            xs_ref.at[pl.ds((s // nc) * R, R), pl.ds((s % nc) * C, C), :], xs_buf.at[slot], xs_sem.at[slot])

    @pl.when(step == 0)
    def _():
        for s in range(nbuf - 1):
            @pl.when(s < total)
            def _():
                block_copy(s).start()

    @pl.when(step + nbuf - 1 < total)
    def _():
        block_copy(step + nbuf - 1).start()

    block_copy(step).wait()
    xs_cur = xs_buf.at[step % nbuf]

    @pl.when(pl.program_id(1) == 0)
    def _():
        state_ref[...] = jnp.zeros_like(state_ref)

    ones_bd = hs_ref[...]
    GW = ones_bd.shape[0]
    NG = W // GW
    bd_mask = (lax.broadcasted_iota(jnp.int32, (GW, GW), 0) // N
               == lax.broadcasted_iota(jnp.int32, (GW, GW), 1) // N)
    rowi = lax.broadcasted_iota(jnp.int32, (C, GW), 0)
    colh = lax.broadcasted_iota(jnp.int32, (C, GW), 1) % N
    strict = rowi > colh
    incl = rowi >= colh

    def head_sum(x, prec):
        n = x.shape[0]
        xs_ = jnp.concatenate([x[:, i * GW:(i + 1) * GW] for i in range(NG)], axis=0)
        s = _pdot(_split(xs_, prec), [ones_bd])
        return jnp.concatenate([s[i * n:(i + 1) * n] for i in range(NG)], axis=1)

    def prepare(b0, nb):
        n = nb * C
        xs = xs_cur[b0:b0 + nb].reshape(n, xs_cur.shape[2])
        r = xs[:, 0:W]
        k = xs[:, W:2 * W]
        v = xs[:, 2 * W:3 * W]
        wd = xs[:, 3 * W:3 * W + DECAY_LORA]
        ad = xs[:, 3 * W + DECAY_LORA:3 * W + DECAY_LORA + AAA_LORA]
        gd = xs[:, 3 * W + DECAY_LORA + AAA_LORA:]
        logd = -DECAY_SCALE * jax.nn.sigmoid(w0_ref[...] + _dot(jnp.tanh(wd), w2_ref[...]))
        a = jax.nn.sigmoid(a0_ref[...] + _dot(ad, a2_ref[...]))
        g = _dot(jax.nn.sigmoid(gd), g2_ref[...])
        kk = k * kk_ref[...]
        kk = kk * lax.rsqrt(jnp.maximum(head_sum(kk * kk, PREC_KNORM), KNORM_FLOOR * KNORM_FLOOR))
        k = k * (1.0 + (a - 1.0) * ka_ref[...])
        if has_vmix:
            gate = jax.nn.sigmoid(v0_ref[...] + _dot(_dot(v, v1_ref[...]), v2_ref[...]))
            v = v + (vfirst_ref[b0:b0 + nb].reshape(n, W) - v) * gate
        else:
            vout_ref[b0:b0 + nb] = v.reshape(nb, C, W)
        brow = lax.broadcasted_iota(jnp.int32, (n, n), 0)
        bcol = lax.broadcasted_iota(jnp.int32, (n, n), 1)
        tri = ((brow >= bcol) & (brow // C == bcol // C)).astype(CHUNK_DT)
        cum = _pdot([tri], _split(logd, PREC_CUM))
        grow = jnp.exp(-cum)
        kka = kk * a
        return dict(
            decay_end=[jnp.exp(cum[(b + 1) * C - 1:(b + 1) * C, :]) for b in range(nb)],
            r_t=r * jnp.exp(cum), a_t=-kk * jnp.exp(cum - logd), k_t=k * grow, b_t=kka * grow,
            v=v, g=g, rk=r * k * rk_ref[...])

    def recur(p, b0, nb, after_inverse=None):
        slabs = [(b, i) for b in range(nb) for i in range(NG)]

        def cut(x, b, i):
            return x[b * C:(b + 1) * C, i * GW:(i + 1) * GW]

        ar = [jnp.concatenate([cut(p["a_t"], b, i), cut(p["r_t"], b, i)], axis=0) for b, i in slabs]
        am_b = [_gdot(x, cut(p["b_t"], b, i), bd_mask, PREC_A, "nt") for x, (b, i) in zip(ar, slabs)]
        am_k = [_gdot(x, cut(p["k_t"], b, i), bd_mask, PREC_A, "nt") for x, (b, i) in zip(ar, slabs)]
        tinvs = _unit_lower_inverse([jnp.where(strict, m[:C], 0.0) for m in am_b], rowi, colh, bd_mask, PREC_INV)
        if after_inverse is not None:
            after_inverse()
        a_ak_rk =[jnp.concatenate([jnp.where(strict, m[:C], 0.0), jnp.where(incl, m[C:], 0.0)], axis=0)
                   for m in am_k]
        a_rb = [jnp.where(incl, m[C:], 0.0) for m in am_b]
        sts = [state_ref[b0 + b, i] for b, i in slabs]
        carry = [_gdot(x, st, bd_mask, PREC_APPLY, "nt") for x, st in zip(ar, sts)]
        intra = [_gdot(x, cut(p["v"], b, i), bd_mask, PREC_APPLY) for x, (b, i) in zip(a_ak_rk, slabs)]
        us = [_gdot(ti, cr[:C] + it[:C], bd_mask, PREC_APPLY)
              for ti, cr, it in zip(tinvs, carry, intra)]
        ys = [cr[C:] + it[C:] + _gdot(rb, u, bd_mask, PREC_APPLY)
              for cr, it, rb, u in zip(carry, intra, a_rb, us)]
        for (b, i), st, u in zip(slabs, sts, us):
            uv = jnp.concatenate([u, cut(p["v"], b, i)], axis=0)
            dend = p["decay_end"][b][:, i * GW:(i + 1) * GW]
            bk = jnp.concatenate([cut(p["b_t"], b, i), cut(p["k_t"], b, i)], axis=0) * dend
            full = _pdot(_split(uv, PREC_APPLY), _split(bk, PREC_APPLY), "tn")
            full = jnp.where(bd_mask, full, 0.0)
            st = dend * st
            for h in range(GROUP):
                st = st + full[h * N:(h + 1) * N, :]
            state_ref[b0 + b, i] = st
        return jnp.concatenate([jnp.concatenate(ys[b * NG:(b + 1) * NG], axis=1) for b in range(nb)], axis=0)

    def finish(p, y, b0, nb):
        mean = head_sum(y, PREC_SUM) * (1.0 / N)
        yc = y - mean
        var = head_sum(yc * yc, PREC_SUM) * (1.0 / N)
        y = yc * lax.rsqrt(var + LNX_EPS) * lng_ref[...] + lnb_ref[...]
        bonus = head_sum(p["rk"], PREC_SUM) * p["v"]
        out_ref[b0:b0 + nb] = ((y + bonus) * p["g"]).reshape(nb, C, W)

    parts = [(0, R // 2), (R // 2, R - R // 2)] if R >= 2 * RWKV_ROWS_PER_PART else [(0, R)]
    prepped = [prepare(*parts[0])]
    ys = []
    for n, (b0, nb) in enumerate(parts):
        nxt = (lambda m=n + 1: prepped.append(prepare(*parts[m]))) if n + 1 < len(parts) else None
        ys.append(recur(prepped[n], b0, nb, nxt))
    for p, y, (b0, nb) in zip(prepped, ys, parts):
        finish(p, y, b0, nb)


def _rwkv(rest, vfirst, p, batch, seq):
    n, shift_width = rest.shape
    nc = seq // CHUNK
    W = RWKV_WIDTH
    rows = RWKV_ROWS if batch % RWKV_ROWS == 0 else 1
    tok = lambda b, c: (b, c, 0)
    const = lambda b, c: (0, 0)
    vec = lambda a: a.reshape(1, -1)
    has_vmix = vfirst is not None
    full = lambda a: pl.BlockSpec(a.shape, const)
    gw = GROUP * HEAD_DIM
    hidx = jnp.arange(gw) // HEAD_DIM
    bd_mask = (hidx[:, None] == hidx[None, :]).astype(BF16)

    args = [rest.reshape(batch, seq, shift_width)]
    specs = [pl.BlockSpec(memory_space=pl.ANY)]
    if has_vmix:
        args.append(vfirst.reshape(batch, seq, W))
        specs.append(pl.BlockSpec((rows, CHUNK, W), tok))
    params = [bd_mask, vec(p["w0"]), p["w2"], vec(p["a0"]), p["a2"], p["g2"],
              vec(p["k_k"]), vec(p["k_a"]), vec(p["r_k"]), vec(p["lnx_g"]), vec(p["lnx_b"])]
    if has_vmix:
        params += [vec(p["v0"]), p["v1"], p["v2"]]
    args += params
    specs += [full(a) for a in params]

    out_spec = pl.BlockSpec((rows, CHUNK, W), tok)
    out_sds = jax.ShapeDtypeStruct((batch, seq, W), F32)
    outs = pl.pallas_call(
        functools.partial(_rwkv_kernel, has_vmix),
        grid=(batch // rows, nc),
        in_specs=specs,
        out_specs=out_spec if has_vmix else [out_spec, out_spec],
        out_shape=out_sds if has_vmix else [out_sds, out_sds],
        scratch_shapes=[pltpu.VMEM((rows, W // gw, HEAD_DIM, gw), F32),
                        pltpu.VMEM((RWKV_INPUT_BUFFERS, rows, CHUNK, shift_width), F32),
                        pltpu.SemaphoreType.DMA((RWKV_INPUT_BUFFERS,))],
        compiler_params=pltpu.CompilerParams(
            dimension_semantics=("arbitrary", "arbitrary"), vmem_limit_bytes=VMEM_LIMIT),
        name="rwkv_vmix" if has_vmix else "rwkv",
    )(*args)
    if has_vmix:
        return outs.reshape(n, W), vfirst
    return outs[0].reshape(n, W), outs[1].reshape(n, W)


def _ffn_kernel(final, *refs):
    if final:
        (x_ref, att_ref, rw_ref, wo_ref, n2g_ref, wg_ref, wu_ref, cw_ref, cb_ref, wd_ref, fg_ref,
         out_ref, halo_ref) = refs
    else:
        (x_ref, att_ref, rw_ref, wo_ref, n2g_ref, wg_ref, wu_ref, cw_ref, cb_ref, wd_ref,
         out_ref, halo_ref) = refs
    tm = x_ref.shape[0]

    @pl.when(pl.program_id(1) == 0)
    def _():
        halo_ref[...] = jnp.zeros_like(halo_ref)

    x = (x_ref[...] + _dot(att_ref[...], wo_ref[:ATT_WIDTH, :])
         + _dot(rw_ref[...], wo_ref[ATT_WIDTH:, :]))
    h = _rms_norm(x, n2g_ref[...]).astype(BF16)
    u = jnp.dot(h, wg_ref[...], preferred_element_type=F32)
    up = jnp.dot(h, wu_ref[...], preferred_element_type=F32)
    ext = jnp.concatenate([halo_ref[...], u], axis=0)
    halo_ref[...] = u[tm - SUBLANES:, :]
    c = (cb_ref[...] + cw_ref[0:1, :] * ext[SUBLANES - 2:SUBLANES - 2 + tm, :]
         + cw_ref[1:2, :] * ext[SUBLANES - 1:SUBLANES - 1 + tm, :] + cw_ref[2:3, :] * u)
    act = (c * jax.nn.sigmoid(c)) * up
    x = x + _dot(act, wd_ref[...])
    if final:
        x = _rms_norm(x, fg_ref[...])
    out_ref[...] = x


def _ffn(xf, att, rw, layer, wo, n2g, wg, wu, cw, cb, wd, fg, batch, seq, tm):
    n, d = xf.shape
    ff = wg.shape[2]
    nt = seq // tm
    tok = lambda b, t: (b * nt + t, 0)
    const = lambda b, t: (0, 0)
    resident = lambda shape: pl.BlockSpec(shape, const, pipeline_mode=pl.Buffered(1))
    stacked = lambda w: pl.BlockSpec((None,) + w.shape[1:], lambda b, t: (layer, 0, 0),
                                     pipeline_mode=pl.Buffered(1))
    final = fg is not None
    args = [xf, att, rw, wo, n2g.reshape(1, d), wg, wu, cw, cb.reshape(1, ff), wd]
    specs = [pl.BlockSpec((tm, d), tok),
             pl.BlockSpec((tm, ATT_WIDTH), tok),
             pl.BlockSpec((tm, RWKV_WIDTH), tok),
             stacked(wo), resident((1, d)), stacked(wg), stacked(wu),
             resident(cw.shape), resident((1, ff)), stacked(wd)]
    if final:
        args.append(fg.reshape(1, d))
        specs.append(resident((1, d)))
    return pl.pallas_call(
        functools.partial(_ffn_kernel, final),
        grid=(batch, nt),
        in_specs=specs,
        out_specs=pl.BlockSpec((tm, d), tok),
        out_shape=jax.ShapeDtypeStruct((n, d), F32),
        scratch_shapes=[pltpu.VMEM((SUBLANES, ff), F32)],
        compiler_params=pltpu.CompilerParams(
            dimension_semantics=("arbitrary", "arbitrary"), vmem_limit_bytes=VMEM_LIMIT),
        name="ffn_final" if final else "ffn",
    )(*args)


def kernel(x, norm1_g, w_in, attn_sinks, shift_mu, w0, w2, a0, a2, g2, k_k, k_a, r_k, lnx_g, lnx_b,
           v0, v1, v2, w_out, norm2_g, ffn_w_gate, ffn_w_up, conv_w, conv_b, ffn_w_down, final_g):
    batch, seq, d = x.shape
    depth = w_in.shape[0]
    assert seq % WINDOW == 0 and seq % CHUNK == 0
    tm_proj = tm_ffn = min(ROW_TILE, seq)
    xf = x.reshape(batch * seq, d)
    w_in, w_out, ffn_w_gate, ffn_w_up, ffn_w_down = (
        w.astype(BF16) for w in (w_in, w_out, ffn_w_gate, ffn_w_up, ffn_w_down))
    vfirst = None
    for l in range(depth):
        att, rest = _proj_attn(xf, norm1_g[l], w_in, l, shift_mu[l], attn_sinks[l], seq, tm_proj)
        p = dict(w0=w0[l], w2=w2[l], a0=a0[l], a2=a2[l], g2=g2[l],
                 k_k=k_k[l], k_a=k_a[l], r_k=r_k[l], lnx_g=lnx_g[l], lnx_b=lnx_b[l])
        if l > 0:
            p.update(v0=v0[l - 1], v1=v1[l - 1], v2=v2[l - 1])
        rw, vfirst = _rwkv(rest, vfirst, p, batch, seq)
        xf = _ffn(xf, att, rw, l, w_out, norm2_g[l], ffn_w_gate, ffn_w_up, conv_w[l], conv_b[l], ffn_w_down,
                  final_g if l == depth - 1 else None, batch, seq, tm_ffn)
    return xf.reshape(batch, seq, d)
```
